```python
import jax, jax.numpy as jnp
from jax import lax
import numpy as np

D_MODEL = 1024
BATCH = 16
SEQ = 2048
DEPTH = 4

CTX_LEN = 256
GRID_W = 64
NA_HEADS = 8
NA_HEAD_DIM = 64
NA_WIN_R = 8
NA_WIN_C = 16
NA_WIDTH = NA_HEADS * NA_HEAD_DIM
RPB_R = 2 * NA_WIN_R - 1
RPB_C = 2 * NA_WIN_C - 1
MLA_HEADS = 8
MLA_Q_RANK = 256
MLA_KV_RANK = 128
MLA_NOPE_DIM = 64
MLA_ROPE_DIM = 32
MLA_V_DIM = 64
MLA_WIDTH = MLA_HEADS * MLA_V_DIM
ATTN_BLOCK = 128
ROPE_BASE = 10000.0
CONV_DIM = 512
CONV_WIDTH = 31
N_BRANCHES = 3
N_EXPERTS = 16
EXPERT_DIM = 1024
EC_CAPACITY_FACTOR = 2
LN_EPS = 1e-5
RMS_EPS = 1e-6
DEEPNORM_ALPHA = (2 * DEPTH) ** 0.25
DEEPNORM_BETA = (8 * DEPTH) ** -0.25
PROJ_SIZES = (NA_WIDTH, NA_WIDTH, NA_WIDTH, MLA_Q_RANK, MLA_KV_RANK, MLA_ROPE_DIM, 2 * CONV_DIM, N_BRANCHES * D_MODEL)
IN_COLS = sum(PROJ_SIZES)

kernel_name = "hybrid_na_mla_conformer_ecmoe_dit"


def _layer_norm(x, g, b):
    xf = x.astype(jnp.float32)
    mu = jnp.mean(xf, axis=-1, keepdims=True)
    var = jnp.mean(jnp.square(xf - mu), axis=-1, keepdims=True)
    return ((xf - mu) * lax.rsqrt(var + LN_EPS) * g + b).astype(x.dtype)


def _rms_norm(x, g):
    xf = x.astype(jnp.float32)
    return (xf * lax.rsqrt(jnp.mean(jnp.square(xf), axis=-1, keepdims=True) + RMS_EPS) * g).astype(x.dtype)


def _modulate(x, shift, scale):
    return x * (1 + scale) + shift


def _axial_angles(n_tokens):
    t = jnp.arange(n_tokens)
    row = (t // GRID_W).astype(jnp.float32)
    col = (t % GRID_W).astype(jnp.float32)
    n_freq = MLA_ROPE_DIM // 4
    inv_freq = ROPE_BASE ** (-jnp.arange(n_freq, dtype=jnp.float32) / n_freq)
    return row[:, None] * inv_freq, col[:, None] * inv_freq


def _rotate(x, ang):
    x1, x2 = jnp.split(x, 2, axis=-1)
    cos = jnp.cos(ang).astype(x.dtype)
    sin = jnp.sin(ang).astype(x.dtype)
    return jnp.concatenate([x1 * cos - x2 * sin, x2 * cos + x1 * sin], axis=-1)


def _rope_2d(x, ang_row, ang_col):
    xr, xc = jnp.split(x, 2, axis=-1)
    return jnp.concatenate([_rotate(xr, ang_row), _rotate(xc, ang_col)], axis=-1)


def _split_proj(p):
    offsets = np.cumsum(PROJ_SIZES)[:-1].tolist()
    return jnp.split(p, offsets, axis=-1)


def _heads(t, n_heads):
    return t.reshape(t.shape[0], t.shape[1], n_heads, t.shape[2] // n_heads)


def _dense_attention(q, k, v, scale):
    s = jnp.einsum('bqhd,bkhd->bhqk', q, k).astype(jnp.float32) * scale
    p = jax.nn.softmax(s, axis=-1).astype(v.dtype)
    o = jnp.einsum('bhqk,bkhd->bqhd', p, v)
    return o.reshape(o.shape[0], o.shape[1], -1)


def _blocked_attention(q, k, v, scale):
    B, S, H, Dq = q.shape
    nb = S // ATTN_BLOCK
    qb = q.reshape(B, nb, ATTN_BLOCK, H, Dq).transpose(1, 0, 2, 3, 4)

    def block(q_blk):
        s = jnp.einsum('bqhd,bkhd->bhqk', q_blk, k).astype(jnp.float32) * scale
        p = jax.nn.softmax(s, axis=-1).astype(v.dtype)
        return jnp.einsum('bhqk,bkhd->bqhd', p, v)

    o = lax.map(block, qb)
    return o.transpose(1, 0, 2, 3, 4).reshape(B, S, -1)


def _neighbourhood_attention(q, k, v, k_ctx, v_ctx, rpb):
    B, S, H, Dh = q.shape
    rows = S // GRID_W
    win_r = min(NA_WIN_R, rows)
    scale = Dh ** -0.5
    qg = q.reshape(B, rows, GRID_W, H, Dh)
    kg = k.reshape(B, rows, GRID_W, H, Dh)
    vg = v.reshape(B, rows, GRID_W, H, Dh)
    col = jnp.arange(GRID_W)
    c_start = jnp.clip(col - NA_WIN_C // 2, 0, GRID_W - NA_WIN_C)
    col_mask = (col[None, :] >= c_start[:, None]) & (col[None, :] < c_start[:, None] + NA_WIN_C)
    dc_idx = jnp.clip(col[None, :] - col[:, None], -(NA_WIN_C - 1), NA_WIN_C - 1) + NA_WIN_C - 1
    n_win = win_r * GRID_W

    def row_block(r):
        r_start = jnp.clip(r - win_r // 2, 0, rows - win_r)
        q_r = lax.dynamic_index_in_dim(qg, r, axis=1, keepdims=False)
        k_band = lax.dynamic_slice_in_dim(kg, r_start, win_r, axis=1)
        v_band = lax.dynamic_slice_in_dim(vg, r_start, win_r, axis=1)
        dr_idx = r_start + jnp.arange(win_r) - r + NA_WIN_R - 1
        bias = rpb[:, dr_idx[None, :, None], dc_idx[:, None, :]]
        s_win = jnp.einsum('bqhd,brkhd->bhqrk', q_r, k_band).astype(jnp.float32) * scale + bias
        s_win = jnp.where(col_mask[:, None, :], s_win, -jnp.inf).reshape(B, H, GRID_W, n_win)
        s_ctx = jnp.einsum('bqhd,bkhd->bhqk', q_r, k_ctx).astype(jnp.float32) * scale
        p = jax.nn.softmax(jnp.concatenate([s_win, s_ctx], axis=-1), axis=-1).astype(v.dtype)
        p_win = p[..., :n_win].reshape(B, H, GRID_W, win_r, GRID_W)
        p_ctx = p[..., n_win:]
        return (jnp.einsum('bhqrk,brkhd->bqhd', p_win, v_band)
                + jnp.einsum('bhqk,bkhd->bqhd', p_ctx, v_ctx))

    o = lax.map(row_block, jnp.arange(rows))
    return o.transpose(1, 0, 2, 3, 4).reshape(B, S, H * Dh)


def _mla_queries(c_q, g_q, w_uq, ang_row, ang_col):
    B, N, _ = c_q.shape
    q = (_rms_norm(c_q, g_q) @ w_uq).reshape(B, N, MLA_HEADS, MLA_NOPE_DIM + MLA_ROPE_DIM)
    if ang_row is None:
        return q
    q_rope = _rope_2d(q[..., MLA_NOPE_DIM:], ang_row[:, None, :], ang_col[:, None, :])
    return jnp.concatenate([q[..., :MLA_NOPE_DIM], q_rope], axis=-1)


def _mla_keys_values(c_kv, k_rope, g_kv, w_ukv, ang_row, ang_col):
    B, N, _ = c_kv.shape
    kv = (_rms_norm(c_kv, g_kv) @ w_ukv).reshape(B, N, MLA_HEADS, MLA_NOPE_DIM + MLA_V_DIM)
    k_nope, v = kv[..., :MLA_NOPE_DIM], kv[..., MLA_NOPE_DIM:]
    k_rope = k_rope[:, :, None, :]
    if ang_row is not None:
        k_rope = _rope_2d(k_rope, ang_row[:, None, :], ang_col[:, None, :])
    k = jnp.concatenate([k_nope, jnp.broadcast_to(k_rope, (B, N, MLA_HEADS, MLA_ROPE_DIM))], axis=-1)
    return k, v


def _conformer_conv(u2, w_dw, b_dw, g_cn, b_cn, w_pw2):
    a, gate = jnp.split(u2, 2, axis=-1)
    u = a * jax.nn.sigmoid(gate)
    u = lax.conv_general_dilated(u, w_dw[:, None, :], window_strides=(1,),
                                 padding=[(CONV_WIDTH // 2, CONV_WIDTH // 2)],
                                 dimension_numbers=('NWC', 'WIO', 'NWC'),
                                 feature_group_count=u.shape[-1]) + b_dw
    u = jax.nn.silu(_layer_norm(u, g_cn, b_cn))
    return u @ w_pw2


def _merge(y_a, y_b, y_c, gate_logits, w_oa, w_ob, w_out):
    g = jax.nn.sigmoid(gate_logits.reshape(gate_logits.shape[:-1] + (N_BRANCHES, D_MODEL)))
    m = g[..., 0, :] * (y_a @ w_oa) + g[..., 1, :] * (y_b @ w_ob) + g[..., 2, :] * y_c
    return m @ w_out


def _mixer(h_lat, h_ctx, lp, ang_row, ang_col, with_ctx_out):
    qa, ka, va, cq, ckv, krope, conv_in, gates = _split_proj(h_lat @ lp['w_in'])
    qa_c, ka_c, va_c, cq_c, ckv_c, krope_c, conv_in_c, gates_c = _split_proj(h_ctx @ lp['w_in'])
    ka_c, va_c = _heads(ka_c, NA_HEADS), _heads(va_c, NA_HEADS)
    k_bc, v_bc = _mla_keys_values(ckv_c, krope_c, lp['g_kv'], lp['w_ukv'], None, None)
    mla_scale = (MLA_NOPE_DIM + MLA_ROPE_DIM) ** -0.5
    y_a = _neighbourhood_attention(_heads(qa, NA_HEADS), _heads(ka, NA_HEADS), _heads(va, NA_HEADS),
                                   ka_c, va_c, lp['rpb'])
    q_b = _mla_queries(cq, lp['g_q'], lp['w_uq'], ang_row, ang_col)
    k_b, v_b = _mla_keys_values(ckv, krope, lp['g_kv'], lp['w_ukv'], ang_row, ang_col)
    y_b = _blocked_attention(q_b, jnp.concatenate([k_bc, k_b], axis=1), jnp.concatenate([v_bc, v_b], axis=1), mla_scale)
    y_c = _conformer_conv(conv_in, lp['w_dw'], lp['b_dw'], lp['g_cn'], lp['b_cn'], lp['w_pw2'])
    y_lat = _merge(y_a, y_b, y_c, gates, lp['w_oa'], lp['w_ob'], lp['w_out'])
    if not with_ctx_out:
        return y_lat, None
    y_a_c = _dense_attention(_heads(qa_c, NA_HEADS), ka_c, va_c, NA_HEAD_DIM ** -0.5)
    q_bc = _mla_queries(cq_c, lp['g_q'], lp['w_uq'], None, None)
    y_b_c = _dense_attention(q_bc, k_bc, v_bc, mla_scale)
    y_c_c = _conformer_conv(conv_in_c, lp['w_dw'], lp['b_dw'], lp['g_cn'], lp['b_cn'], lp['w_pw2'])
    y_ctx = _merge(y_a_c, y_b_c, y_c_c, gates_c, lp['w_oa'], lp['w_ob'], lp['w_out'])
    return y_lat, y_ctx


def _expert_choice_moe(h, w_router, w_gate, w_up, w_down):
    B, N, _ = h.shape
    cap = EC_CAPACITY_FACTOR * N // N_EXPERTS
    aff = jax.nn.softmax((h @ w_router).astype(jnp.float32), axis=-1)
    top_aff, top_idx = lax.top_k(aff.transpose(0, 2, 1), cap)
    b_idx = jnp.arange(B)[:, None, None]
    xg = h[b_idx, top_idx]
    hid = jax.nn.silu(jnp.einsum('becd,edf->becf', xg, w_gate)) * jnp.einsum('becd,edf->becf', xg, w_up)
    y = jnp.einsum('becf,efd->becd', hid, w_down) * top_aff[..., None].astype(h.dtype)
    return jnp.zeros_like(h).at[b_idx, top_idx].add(y)


def setup_inputs(seed: int = 0) -> dict:
    key = jax.random.key(seed)
    ks = iter(jax.random.split(key, 32))

    def nrm(shape, scale):
        return jax.random.normal(next(ks), shape, jnp.float32) * scale

    def gain(shape):
        return 1.0 + nrm(shape, 0.01)

    L, D = DEPTH, D_MODEL
    return {
        "x": nrm((BATCH, SEQ, D), 1.0),
        "c": nrm((BATCH, D), 1.0),
        "ctx": nrm((BATCH, CTX_LEN, D), 1.0),
        "c_ctx": nrm((D,), 1.0),
        "w_ada": nrm((L, D, 6 * D), 0.5 * D ** -0.5),
        "b_ada": nrm((L, 6 * D), 0.01),
        "w_in": nrm((L, D, IN_COLS), D ** -0.5),
        "g_q": gain((L, MLA_Q_RANK)),
        "w_uq": nrm((L, MLA_Q_RANK, MLA_HEADS * (MLA_NOPE_DIM + MLA_ROPE_DIM)), MLA_Q_RANK ** -0.5),
        "g_kv": gain((L, MLA_KV_RANK)),
        "w_ukv": nrm((L, MLA_KV_RANK, MLA_HEADS * (MLA_NOPE_DIM + MLA_V_DIM)), MLA_KV_RANK ** -0.5),
        "rpb": nrm((L, NA_HEADS, RPB_R, RPB_C), 0.1),
        "w_dw": nrm((L, CONV_WIDTH, CONV_DIM), CONV_WIDTH ** -0.5),
        "b_dw": nrm((L, CONV_DIM), 0.01),
        "g_cn": gain((L, CONV_DIM)),
        "b_cn": nrm((L, CONV_DIM), 0.01),
        "w_pw2": nrm((L, CONV_DIM, D), CONV_DIM ** -0.5),
        "w_oa": nrm((L, NA_WIDTH, D), NA_WIDTH ** -0.5),
        "w_ob": nrm((L, MLA_WIDTH, D), MLA_WIDTH ** -0.5),
        "w_out": nrm((L, D, D), DEEPNORM_BETA * D ** -0.5),
        "ln1_g": gain((L, D)),
        "ln1_b": nrm((L, D), 0.01),
        "w_router": nrm((L, D, N_EXPERTS), D ** -0.5),
        "w_gate": nrm((L, N_EXPERTS, D, EXPERT_DIM), D ** -0.5),
        "w_up": nrm((L, N_EXPERTS, D, EXPERT_DIM), D ** -0.5),
        "w_down": nrm((L, N_EXPERTS, EXPERT_DIM, D), DEEPNORM_BETA * EXPERT_DIM ** -0.5),
        "ln2_g": gain((L, D)),
        "ln2_b": nrm((L, D), 0.01),
    }


def reference(x, c, ctx, c_ctx, w_ada, b_ada, w_in, g_q, w_uq, g_kv, w_ukv, rpb, w_dw, b_dw, g_cn, b_cn,
              w_pw2, w_oa, w_ob, w_out, ln1_g, ln1_b, w_router, w_gate, w_up, w_down, ln2_g, ln2_b):
    ang_row, ang_col = _axial_angles(x.shape[1])
    for l in range(DEPTH):
        with_ctx_out = l < DEPTH - 1
        lp = dict(w_in=w_in[l], g_q=g_q[l], w_uq=w_uq[l], g_kv=g_kv[l], w_ukv=w_ukv[l], rpb=rpb[l],
                  w_dw=w_dw[l], b_dw=b_dw[l], g_cn=g_cn[l], b_cn=b_cn[l], w_pw2=w_pw2[l],
                  w_oa=w_oa[l], w_ob=w_ob[l], w_out=w_out[l])
        sh1, sc1, g1, sh2, sc2, g2 = jnp.split((jax.nn.silu(c) @ w_ada[l] + b_ada[l])[:, None, :], 6, axis=-1)
        sh1c, sc1c, g1c, sh2c, sc2c, g2c = jnp.split(jax.nn.silu(c_ctx) @ w_ada[l] + b_ada[l], 6, axis=-1)
        y_lat, y_ctx = _mixer(_modulate(x, sh1, sc1), _modulate(ctx, sh1c, sc1c), lp, ang_row, ang_col, with_ctx_out)
        x = _layer_norm(DEEPNORM_ALPHA * x + g1 * y_lat, ln1_g[l], ln1_b[l])
        x = _layer_norm(DEEPNORM_ALPHA * x + g2 * _expert_choice_moe(_modulate(x, sh2, sc2), w_router[l], w_gate[l], w_up[l], w_down[l]),
                        ln2_g[l], ln2_b[l])
        if with_ctx_out:
            ctx = _layer_norm(DEEPNORM_ALPHA * ctx + g1c * y_ctx, ln1_g[l], ln1_b[l])
            ctx = _layer_norm(DEEPNORM_ALPHA * ctx + g2c * _expert_choice_moe(_modulate(ctx, sh2c, sc2c), w_router[l], w_gate[l], w_up[l], w_down[l]),
                              ln2_g[l], ln2_b[l])
    return x
```

```python
import functools

import jax
import jax.numpy as jnp
from jax import lax
from jax.experimental import pallas as pl
from jax.experimental.pallas import tpu as pltpu

F32 = jnp.float32
BF16 = jnp.bfloat16

D_MODEL = 1024
DEPTH = 4
GRID_W = 64
NA_HEADS = 8
NA_HEAD_DIM = 64
NA_WIN_R = 8
NA_WIN_C = 16
NA_WIDTH = NA_HEADS * NA_HEAD_DIM
MLA_HEADS = 8
MLA_Q_RANK = 256
MLA_KV_RANK = 128
MLA_NOPE_DIM = 64
MLA_ROPE_DIM = 32
MLA_V_DIM = 64
ROPE_BASE = 10000.0
CONV_DIM = 512
CONV_WIDTH = 31
N_EXPERTS = 16
EXPERT_DIM = 1024
EC_CAPACITY_FACTOR = 2
LN_EPS = 1e-5
RMS_EPS = 1e-6
DEEPNORM_ALPHA = (2 * DEPTH) ** 0.25

LANES = 128
SUBLANES = 8
VMEM_LIMIT_BYTES = 56 * 1024 * 1024

ROW_TILE = 256
NA_Q_ROWS = 4
NA_BAND_ROWS = NA_Q_ROWS + NA_WIN_R
CONV_CHUNK = 64
CONV_HALO = 16
HEAD_LANES = 128
NEG_BIG = -1e30

C_QA, C_KA, C_VA = 0, 512, 1024
C_CQ, C_CKV, C_KR = 1536, 1792, 1920
C_CONV_A, C_CONV_G, C_GATES = 2048, 2560, 3072
IN_COLS_PAD = 6144
NA_TAB_PAD = 5
NA_TAB = 2 * NA_WIN_R - 1 + 2 * NA_TAB_PAD - 1


def _params(*sem):
    return pltpu.CompilerParams(dimension_semantics=sem, vmem_limit_bytes=VMEM_LIMIT_BYTES)


def _dot(a, b):
    return jnp.dot(a, b, preferred_element_type=F32)


def _dot_nt(a, b):
    return lax.dot_general(a, b, (((1,), (1,)), ((), ())), preferred_element_type=F32)


def _layer_norm(x, g, b):
    mu = jnp.mean(x, axis=-1, keepdims=True)
    var = jnp.mean(jnp.square(x - mu), axis=-1, keepdims=True)
    return (x - mu) * lax.rsqrt(var + LN_EPS) * g + b


def _rms_norm(x, g):
    return x * lax.rsqrt(jnp.mean(jnp.square(x), axis=-1, keepdims=True) + RMS_EPS) * g


def _silu(x):
    return x * jax.nn.sigmoid(x)


def _ada_kernel(c_ref, w_ref, b_ref, o_ref):
    s = _silu(c_ref[...]).astype(BF16)
    o_ref[0] = _dot(s, w_ref[0].astype(BF16)) + b_ref[0]


def _ada_all_layers(cc, w_ada, b_ada):
    n_layers, d, cols = w_ada.shape
    rows = cc.shape[0]
    tn = 1536
    return pl.pallas_call(
        _ada_kernel,
        grid=(n_layers, cols // tn),
        in_specs=[pl.BlockSpec((rows, d), lambda l, j: (0, 0)),
                  pl.BlockSpec((1, d, tn), lambda l, j: (l, 0, j)),
                  pl.BlockSpec((1, 1, tn), lambda l, j: (l, 0, j))],
        out_specs=pl.BlockSpec((1, rows, tn), lambda l, j: (l, 0, j)),
        out_shape=jax.ShapeDtypeStruct((n_layers, rows, cols), F32),
        compiler_params=_params("arbitrary", "arbitrary"),
        name="ada_ln",
    )(cc, w_ada, b_ada.reshape(n_layers, 1, cols))


def _rope(x, c, sa, sb):
    return x * c + pltpu.roll(x, HEAD_LANES - 8, axis=1) * sa + pltpu.roll(x, 8, axis=1) * sb


def _in_proj_kernel(*refs, use_rope, full):
    if use_rope:
        (x_ref, mod_ref, w_ref, gq_ref, gkv_ref, wuq_ref, wuk_ref, wuv_ref, rc_ref, rsa_ref, rsb_ref), outs = refs[:11], refs[11:]
    else:
        (x_ref, mod_ref, w_ref, gq_ref, gkv_ref, wuq_ref, wuk_ref, wuv_ref), outs = refs[:8], refs[8:]
    if full:
        qa_ref, ka_ref, va_ref, qb_ref, kb_ref, vb_ref, u_ref, gl_ref = outs
    else:
        ka_ref, va_ref, kb_ref, vb_ref = outs
    d = D_MODEL
    mod = mod_ref[0]
    h = (x_ref[0] * (1.0 + mod[:, d:2 * d]) + mod[:, 0:d]).astype(BF16)

    def proj(c0, width):
        return _dot(h, w_ref[:, c0:c0 + width])

    ka_ref[0] = proj(C_KA, NA_WIDTH).astype(BF16)
    va_ref[0] = proj(C_VA, NA_WIDTH).astype(BF16)
    if use_rope:
        rc, rsa, rsb = rc_ref[...], rsa_ref[...], rsb_ref[...]
    n_kv = _rms_norm(proj(C_CKV, MLA_KV_RANK), gkv_ref[...]).astype(BF16)
    kr = proj(C_KR, HEAD_LANES)
    if use_rope:
        kr = _rope(kr, rc, rsa, rsb)
    k_nope = _dot(n_kv, wuk_ref[...])
    for hd in range(MLA_HEADS):
        sl = slice(hd * HEAD_LANES, (hd + 1) * HEAD_LANES)
        kb_ref[0, :, sl] = (k_nope[:, sl] + kr).astype(BF16)
    vb_ref[0] = _dot(n_kv, wuv_ref[...]).astype(BF16)
    if not full:
        return
    qa_ref[0] = (proj(C_QA, NA_WIDTH) * (NA_HEAD_DIM ** -0.5)).astype(BF16)
    n_q = _rms_norm(proj(C_CQ, MLA_Q_RANK), gq_ref[...]).astype(BF16)
    q = _dot(n_q, wuq_ref[...])
    mla_scale = (MLA_NOPE_DIM + MLA_ROPE_DIM) ** -0.5
    for hd in range(MLA_HEADS):
        sl = slice(hd * HEAD_LANES, (hd + 1) * HEAD_LANES)
        qh = q[:, sl]
        if use_rope:
            qh = _rope(qh, rc, rsa, rsb)
        qb_ref[0, :, sl] = (qh * mla_scale).astype(BF16)
    u_ref[0] = proj(C_CONV_A, CONV_DIM) * jax.nn.sigmoid(proj(C_CONV_G, CONV_DIM))
    gl_ref[0] = proj(C_GATES, 3 * d)


def _in_proj(x, mod, mod_row, wl, rope, full):
    bsz, n, d = x.shape
    tm = ROW_TILE
    use_rope = rope is not None
    if mod_row is None:
        mod_map = lambda b, i: (b, 0, 0)
    else:
        mod_map = lambda b, i: (mod_row, 0, 0)
    const2 = lambda b, i: (0, 0)
    row_spec = lambda w: pl.BlockSpec((1, tm, w), lambda b, i: (b, i, 0))
    in_specs = [row_spec(d),
                pl.BlockSpec((1, 1, 2 * d), mod_map),
                pl.BlockSpec((d, IN_COLS_PAD), const2, pipeline_mode=pl.Buffered(1)),
                pl.BlockSpec((1, MLA_Q_RANK), const2),
                pl.BlockSpec((1, MLA_KV_RANK), const2),
                pl.BlockSpec((MLA_Q_RANK, MLA_HEADS * HEAD_LANES), const2),
                pl.BlockSpec((MLA_KV_RANK, MLA_HEADS * HEAD_LANES), const2),
                pl.BlockSpec((MLA_KV_RANK, MLA_HEADS * MLA_V_DIM), const2)]
    args = [x, mod, wl["w_in"], wl["g_q"], wl["g_kv"], wl["w_uq"], wl["w_uk"], wl["w_uv"]]
    if use_rope:
        in_specs += [pl.BlockSpec((tm, HEAD_LANES), lambda b, i: (i, 0))] * 3
        args += list(rope)
    sd = lambda w, dt: jax.ShapeDtypeStruct((bsz, n, w), dt)
    kv_shapes = [sd(NA_WIDTH, BF16), sd(NA_WIDTH, BF16), sd(MLA_HEADS * HEAD_LANES, BF16), sd(MLA_HEADS * MLA_V_DIM, BF16)]
    kv_specs = [row_spec(NA_WIDTH), row_spec(NA_WIDTH), row_spec(MLA_HEADS * HEAD_LANES), row_spec(MLA_HEADS * MLA_V_DIM)]
    if full:
        out_shape = [sd(NA_WIDTH, BF16), kv_shapes[0], kv_shapes[1], sd(MLA_HEADS * HEAD_LANES, BF16), kv_shapes[2],
                     kv_shapes[3], sd(CONV_DIM, F32), sd(3 * d, F32)]
        out_specs = [row_spec(NA_WIDTH), kv_specs[0], kv_specs[1], row_spec(MLA_HEADS * HEAD_LANES), kv_specs[2],
                     kv_specs[3], row_spec(CONV_DIM), row_spec(3 * d)]
    else:
        out_shape, out_specs = kv_shapes, kv_specs
    return pl.pallas_call(
        functools.partial(_in_proj_kernel, use_rope=use_rope, full=full),
        grid=(bsz, n // tm),
        in_specs=in_specs,
        out_specs=out_specs,
        out_shape=out_shape,
        compiler_params=_params("parallel", "arbitrary"),
        name="in_proj",
    )(*args)


def _pair_lane_mask(rows):
    return lax.broadcasted_iota(jnp.int32, (rows, LANES), 1) < NA_HEAD_DIM


def _na_kernel(q_ref, k_ref, v_ref, kc_ref, vc_ref, tab_ref, o_ref, *, grid_rows):
    i = pl.program_id(1)
    tq = NA_Q_ROWS * GRID_W
    nband = NA_BAND_ROWS * GRID_W
    r0 = jnp.clip(NA_Q_ROWS * i - NA_WIN_R // 2, 0, grid_rows - NA_BAND_ROWS)
    kstart = pl.multiple_of(r0 * GRID_W, GRID_W)
    qrow = NA_Q_ROWS * i + lax.broadcasted_iota(jnp.int32, (tq, nband), 0) // GRID_W
    krow = r0 + lax.broadcasted_iota(jnp.int32, (tq, nband), 1) // GRID_W
    rstart = jnp.clip(qrow - NA_WIN_R // 2, 0, grid_rows - NA_WIN_R)
    valid = (krow >= rstart) & (krow < rstart + NA_WIN_R)
    tab_off = r0 - NA_Q_ROWS * i + (NA_WIN_R - 1) + NA_TAB_PAD
    first = _pair_lane_mask(tq)
    for p in range(NA_HEADS // 2):
        ls = slice(p * LANES, (p + 1) * LANES)
        qp = q_ref[0, :, ls]
        kband = k_ref[0, pl.ds(kstart, nband), ls]
        vband = v_ref[0, pl.ds(kstart, nband), ls]
        kctx = kc_ref[0, :, ls]
        vctx = vc_ref[0, :, ls]
        halves = []
        for hh in range(2):
            hd = 2 * p + hh
            qm = jnp.where(first if hh == 0 else jnp.logical_not(first), qp, jnp.zeros_like(qp))
            sw = _dot_nt(qm, kband)
            sc = _dot_nt(qm, kctx)
            bias_rows = []
            for a in range(NA_Q_ROWS):
                blocks = [tab_ref[hd * NA_TAB + tab_off + 2 * m - a] for m in range(NA_BAND_ROWS // 2)]
                bias_rows.append(jnp.concatenate(blocks, axis=1))
            sw = jnp.where(valid, sw + jnp.concatenate(bias_rows, axis=0), NEG_BIG)
            mx = jnp.maximum(jnp.max(sw, axis=1, keepdims=True), jnp.max(sc, axis=1, keepdims=True))
            pw = jnp.exp(sw - mx)
            pc = jnp.exp(sc - mx)
            den = jnp.sum(pw, axis=1, keepdims=True) + jnp.sum(pc, axis=1, keepdims=True)
            o = _dot(pw.astype(BF16), vband) + _dot(pc.astype(BF16), vctx)
            halves.append(o / den)
        o_ref[0, :, ls] = jnp.where(first, halves[0], halves[1]).astype(BF16)


def _na_attention(qa, ka, va, ka_c, va_c, tab):
    bsz, s, w = qa.shape
    n_ctx = ka_c.shape[1]
    grid_rows = s // GRID_W
    assert grid_rows >= NA_BAND_ROWS and grid_rows % NA_Q_ROWS == 0
    tq = NA_Q_ROWS * GRID_W
    return pl.pallas_call(
        functools.partial(_na_kernel, grid_rows=grid_rows),
        grid=(bsz, s // tq),
        in_specs=[pl.BlockSpec((1, tq, w), lambda b, i: (b, i, 0)),
                  pl.BlockSpec((1, s, w), lambda b, i: (b, 0, 0)),
                  pl.BlockSpec((1, s, w), lambda b, i: (b, 0, 0)),
                  pl.BlockSpec((1, n_ctx, w), lambda b, i: (b, 0, 0)),
                  pl.BlockSpec((1, n_ctx, w), lambda b, i: (b, 0, 0)),
                  pl.BlockSpec(tab.shape, lambda b, i: (0, 0, 0), pipeline_mode=pl.Buffered(1))],
        out_specs=pl.BlockSpec((1, tq, w), lambda b, i: (b, i, 0)),
        out_shape=jax.ShapeDtypeStruct((bsz, s, w), BF16),
        compiler_params=_params("parallel", "arbitrary"),
        name="na_attn",
    )(qa, ka, va, ka_c, va_c, tab)


def _dense_attn_kernel(*refs, n_seg, paired_q):
    q_ref, kv_refs, o_ref = refs[0], refs[1:1 + 2 * n_seg], refs[1 + 2 * n_seg]
    tq = q_ref.shape[1]
    first = _pair_lane_mask(tq)
    n_heads = o_ref.shape[2] // NA_HEAD_DIM
    for p in range(n_heads // 2):
        ls = slice(p * LANES, (p + 1) * LANES)
        halves = []
        for hh in range(2):
            hd = 2 * p + hh
            if paired_q:
                qp = q_ref[0, :, ls]
                qh = jnp.where(first if hh == 0 else jnp.logical_not(first), qp, jnp.zeros_like(qp))
                ks = ls
            else:
                ks = slice(hd * HEAD_LANES, (hd + 1) * HEAD_LANES)
                qh = q_ref[0, :, ks]
            scores = [_dot_nt(qh, kv_refs[2 * j][0, :, ks]) for j in range(n_seg)]
            mx = functools.reduce(jnp.maximum, [jnp.max(sj, axis=1, keepdims=True) for sj in scores])
            probs = [jnp.exp(sj - mx) for sj in scores]
            den = functools.reduce(lambda a, b: a + b, [jnp.sum(pj, axis=1, keepdims=True) for pj in probs])
            o = functools.reduce(lambda a, b: a + b,
                                 [_dot(probs[j].astype(BF16), kv_refs[2 * j + 1][0, :, ls]) for j in range(n_seg)])
            halves.append(o / den)
        o_ref[0, :, ls] = jnp.where(first, halves[0], halves[1]).astype(BF16)


def _dense_attention(q, kvs, paired_q):
    bsz, nq, wq = q.shape
    tq = ROW_TILE
    wv = kvs[0][1].shape[2]
    in_specs = [pl.BlockSpec((1, tq, wq), lambda b, i: (b, i, 0))]
    args = [q]
    for k, v in kvs:
        in_specs += [pl.BlockSpec((1, k.shape[1], k.shape[2]), lambda b, i: (b, 0, 0)),
                     pl.BlockSpec((1, v.shape[1], v.shape[2]), lambda b, i: (b, 0, 0))]
        args += [k, v]
    return pl.pallas_call(
        functools.partial(_dense_attn_kernel, n_seg=len(kvs), paired_q=paired_q),
        grid=(bsz, nq // tq),
        in_specs=in_specs,
        out_specs=pl.BlockSpec((1, tq, wv), lambda b, i: (b, i, 0)),
        out_shape=jax.ShapeDtypeStruct((bsz, nq, wv), BF16),
        compiler_params=_params("parallel", "arbitrary"),
        name="dense_attn",
    )(*args)


def _conv_kernel(u_ref, w_ref, b_ref, g_ref, bn_ref, z_ref, pad_ref):
    n = u_ref.shape[1]
    half = CONV_WIDTH // 2
    zeros = jnp.zeros((CONV_HALO, CONV_DIM), F32)
    pad_ref[0:CONV_HALO, :] = zeros
    pad_ref[CONV_HALO + n:2 * CONV_HALO + n, :] = zeros
    pad_ref[CONV_HALO:CONV_HALO + n, :] = u_ref[0]

    def chunk(j, carry):
        t0 = pl.multiple_of(j * CONV_CHUNK, CONV_CHUNK)
        win = pad_ref[pl.ds(t0, CONV_CHUNK + 2 * CONV_HALO), :]
        acc = jnp.zeros((CONV_CHUNK, CONV_DIM), F32)
        for k in range(CONV_WIDTH):
            lo = CONV_HALO - half + k
            acc = acc + win[lo:lo + CONV_CHUNK, :] * w_ref[k:k + 1, :]
        y = _layer_norm(acc + b_ref[...], g_ref[...], bn_ref[...])
        z_ref[0, pl.ds(t0, CONV_CHUNK), :] = _silu(y).astype(BF16)
        return carry

    lax.fori_loop(0, n // CONV_CHUNK, chunk, 0)


def _conv_module(u, wl):
    bsz, n, c = u.shape
    vec = pl.BlockSpec((1, c), lambda b: (0, 0))
    return pl.pallas_call(
        _conv_kernel,
        grid=(bsz,),
        in_specs=[pl.BlockSpec((1, n, c), lambda b: (b, 0, 0)),
                  pl.BlockSpec((CONV_WIDTH, c), lambda b: (0, 0)), vec, vec, vec],
        out_specs=pl.BlockSpec((1, n, c), lambda b: (b, 0, 0)),
        out_shape=jax.ShapeDtypeStruct((bsz, n, c), BF16),
        scratch_shapes=[pltpu.VMEM((n + 2 * CONV_HALO, c), F32)],
        compiler_params=_params("parallel"),
        name="conv_module",
    )(u, wl["w_dw"], wl["b_dw"], wl["g_cn"], wl["b_cn"])


def _merge_kernel(x_ref, ya_ref, yb_ref, z_ref, gl_ref, mod_ref, woa_ref, wob_ref, wpw_ref, wout_ref,
                  g_ref, b_ref, wrt_ref, x1_ref, h2_ref, aff_ref):
    d = D_MODEL
    gl = gl_ref[0]
    m = (jax.nn.sigmoid(gl[:, 0:d]) * _dot(ya_ref[0], woa_ref[...])
         + jax.nn.sigmoid(gl[:, d:2 * d]) * _dot(yb_ref[0], wob_ref[...])
         + jax.nn.sigmoid(gl[:, 2 * d:3 * d]) * _dot(z_ref[0], wpw_ref[...]))
    y = _dot(m.astype(BF16), wout_ref[...])
    mod = mod_ref[0]
    x1 = _layer_norm(DEEPNORM_ALPHA * x_ref[0] + mod[:, 0:d] * y, g_ref[...], b_ref[...])
    x1_ref[0] = x1
    h2 = (x1 * (1.0 + mod[:, 2 * d:3 * d]) + mod[:, d:2 * d]).astype(BF16)
    h2_ref[0] = h2
    logits = _dot_nt(wrt_ref[...], h2)
    e = jnp.exp(logits - jnp.max(logits, axis=0, keepdims=True))
    aff_ref[0] = e / jnp.sum(e, axis=0, keepdims=True)


def _merge(x, ya, yb, z, gl, mod, mod_row, wl):
    bsz, n, d = x.shape
    tm = ROW_TILE
    if mod_row is None:
        mod_map = lambda b, i: (b, 0, 0)
    else:
        mod_map = lambda b, i: (mod_row, 0, 0)
    row = lambda w: pl.BlockSpec((1, tm, w), lambda b, i: (b, i, 0))
    const2 = lambda b, i: (0, 0)
    wspec = lambda r: pl.BlockSpec((r, d), const2)
    return pl.pallas_call(
        _merge_kernel,
        grid=(bsz, n // tm),
        in_specs=[row(d), row(NA_WIDTH), row(NA_WIDTH), row(CONV_DIM), row(3 * d),
                  pl.BlockSpec((1, 1, 3 * d), mod_map),
                  wspec(NA_WIDTH), wspec(NA_WIDTH), wspec(CONV_DIM), wspec(d),
                  pl.BlockSpec((1, d), const2), pl.BlockSpec((1, d), const2),
                  pl.BlockSpec((N_EXPERTS, d), const2)],
        out_specs=[row(d), row(d), pl.BlockSpec((1, N_EXPERTS, tm), lambda b, i: (b, 0, i))],
        out_shape=[jax.ShapeDtypeStruct((bsz, n, d), F32), jax.ShapeDtypeStruct((bsz, n, d), BF16),
                   jax.ShapeDtypeStruct((bsz, N_EXPERTS, n), F32)],
        compiler_params=_params("parallel", "arbitrary"),
        name="merge_ln1_router",
    )(x, ya, yb, z, gl, mod, wl["w_oa"], wl["w_ob"], wl["w_pw2"], wl["w_out"], wl["ln1_g"], wl["ln1_b"], wl["w_rt"])


def _lane_cumsum(v):
    n = v.shape[1]
    lane = lax.broadcasted_iota(jnp.int32, v.shape, 1)
    s = 1
    while s < n:
        v = v + jnp.where(lane >= s, pltpu.roll(v, s, axis=1), 0)
        s *= 2
    return v


def _route_kernel(aff_ref, slot_ref, slot_t_ref, *, cap):
    bits = pltpu.bitcast(aff_ref[0], jnp.int32)
    e, n = bits.shape

    def search(it, thr):
        cand = thr | lax.shift_left(jnp.int32(1), 30 - it)
        cnt = jnp.sum(jnp.where(bits >= cand, 1.0, 0.0), axis=1, keepdims=True)
        return jnp.where(cnt >= cap, cand, thr)

    thr = lax.fori_loop(0, 31, search, jnp.zeros((e, 1), jnp.int32))
    gt = jnp.where(bits > thr, 1, 0)
    eq = jnp.where(bits == thr, 1, 0)
    inc = _lane_cumsum(gt + eq * 65536)
    gt_before = (inc & 65535) - gt
    eq_before = lax.shift_right_logical(inc, 16) - eq
    need = cap - jnp.sum(gt, axis=1, keepdims=True)
    sel = (gt == 1) | ((eq == 1) & (eq_before < need))
    slot = jnp.where(sel, gt_before + jnp.minimum(eq_before, need), -1)
    slot_ref[0] = slot
    padded = jnp.concatenate([slot.astype(F32), jnp.full((LANES - e, n), -1.0, F32)], axis=0)
    slot_t_ref[0] = padded.T.astype(jnp.int32)


def _route(aff_t, cap):
    bsz, e, n = aff_t.shape
    return pl.pallas_call(
        functools.partial(_route_kernel, cap=cap),
        grid=(bsz,),
        in_specs=[pl.BlockSpec((1, e, n), lambda b: (b, 0, 0))],
        out_specs=[pl.BlockSpec((1, e, n), lambda b: (b, 0, 0)), pl.BlockSpec((1, n, LANES), lambda b: (b, 0, 0))],
        out_shape=[jax.ShapeDtypeStruct((bsz, e, n), jnp.int32), jax.ShapeDtypeStruct((bsz, n, LANES), jnp.int32)],
        compiler_params=_params("parallel"),
        name="route",
    )(aff_t)


def _expert_kernel(slot_ref, aff_ref, h_ref, wg_ref, wu_ref, wd_ref, y_ref, wg_s, wu_s, wd_s, xg_s, ac_s, *, cap):
    @pl.when(pl.program_id(1) == 0)
    def _():
        wg_s[...] = wg_ref[0].astype(BF16)
        wu_s[...] = wu_ref[0].astype(BF16)
        wd_s[...] = wd_ref[0].astype(BF16)

    nb, n = h_ref.shape[0], h_ref.shape[1]
    c_iota = lax.broadcasted_iota(jnp.int32, (cap, n), 0)
    for j in range(nb):
        hit = slot_ref[j, 0] == c_iota
        onehot = jnp.where(hit, 1.0, 0.0).astype(BF16)
        xg_s[j * cap:(j + 1) * cap, :] = _dot(onehot, h_ref[j]).astype(BF16)
        ac_s[j * cap:(j + 1) * cap, :] = jnp.sum(jnp.where(hit, aff_ref[j, 0], 0.0), axis=1, keepdims=True)
    xg = xg_s[...]
    hid = _silu(_dot(xg, wg_s[...])) * _dot(xg, wu_s[...])
    y = _dot(hid.astype(BF16), wd_s[...]) * ac_s[...]
    for j in range(nb):
        y_ref[j, 0] = y[j * cap:(j + 1) * cap, :].astype(BF16)


def _experts(slot, aff_t, h2, w_gate, w_up, w_down, cap, nb):
    bsz, e, n = slot.shape
    d, f = w_gate.shape[1], w_gate.shape[2]
    slot4 = slot.reshape(bsz, e, 1, n)
    aff4 = aff_t.reshape(bsz, e, 1, n)
    return pl.pallas_call(
        functools.partial(_expert_kernel, cap=cap),
        grid=(e, bsz // nb),
        in_specs=[pl.BlockSpec((nb, 1, 1, n), lambda ei, bi: (bi, ei, 0, 0)),
                  pl.BlockSpec((nb, 1, 1, n), lambda ei, bi: (bi, ei, 0, 0)),
                  pl.BlockSpec((nb, n, d), lambda ei, bi: (bi, 0, 0)),
                  pl.BlockSpec((1, d, f), lambda ei, bi: (ei, 0, 0)),
                  pl.BlockSpec((1, d, f), lambda ei, bi: (ei, 0, 0)),
                  pl.BlockSpec((1, f, d), lambda ei, bi: (ei, 0, 0))],
        out_specs=pl.BlockSpec((nb, 1, cap, d), lambda ei, bi: (bi, ei, 0, 0)),
        out_shape=jax.ShapeDtypeStruct((bsz, e, cap, d), BF16),
        scratch_shapes=[pltpu.VMEM((d, f), BF16), pltpu.VMEM((d, f), BF16), pltpu.VMEM((f, d), BF16),
                        pltpu.VMEM((nb * cap, d), BF16), pltpu.VMEM((nb * cap, 1), F32)],
        compiler_params=_params("arbitrary", "arbitrary"),
        name="experts",
    )(slot4, aff4, h2, w_gate, w_up, w_down)


def _combine_kernel(st_ref, y_ref, x_ref, mod_ref, g_ref, b_ref, o_ref, *, cap):
    st = st_ref[0]
    rows = st.shape[0]
    if cap % LANES == 0:
        c_iota = lax.broadcasted_iota(jnp.int32, (rows, cap), 1)
        scat = jnp.concatenate(
            [jnp.where(st[:, e:e + 1] == c_iota, 1.0, 0.0).astype(BF16) for e in range(N_EXPERTS)], axis=1)
    else:
        j_iota = lax.broadcasted_iota(jnp.int32, (rows, N_EXPERTS * cap), 1)
        hit = None
        for e in range(N_EXPERTS):
            col = st[:, e:e + 1]
            he = jnp.where(col >= 0, col + e * cap, -1) == j_iota
            hit = he if hit is None else (hit | he)
        scat = jnp.where(hit, 1.0, 0.0).astype(BF16)
    moe = _dot(scat, y_ref[0])
    o_ref[0] = _layer_norm(DEEPNORM_ALPHA * x_ref[0] + mod_ref[0] * moe, g_ref[...], b_ref[...])


def _combine(slot_t, y, x1, mod, mod_row, wl, cap):
    bsz, n, d = x1.shape
    tm = ROW_TILE
    if mod_row is None:
        mod_map = lambda b, i: (b, 0, 0)
    else:
        mod_map = lambda b, i: (mod_row, 0, 0)
    ec = N_EXPERTS * cap
    return pl.pallas_call(
        functools.partial(_combine_kernel, cap=cap),
        grid=(bsz, n // tm),
        in_specs=[pl.BlockSpec((1, tm, LANES), lambda b, i: (b, i, 0)),
                  pl.BlockSpec((1, ec, d), lambda b, i: (b, 0, 0)),
                  pl.BlockSpec((1, tm, d), lambda b, i: (b, i, 0)),
                  pl.BlockSpec((1, 1, d), mod_map),
                  pl.BlockSpec((1, d), lambda b, i: (0, 0)), pl.BlockSpec((1, d), lambda b, i: (0, 0))],
        out_specs=pl.BlockSpec((1, tm, d), lambda b, i: (b, i, 0)),
        out_shape=jax.ShapeDtypeStruct((bsz, n, d), F32),
        compiler_params=_params("parallel", "arbitrary"),
        name="combine_ln2",
    )(slot_t, y.reshape(bsz, ec, d), x1, mod, wl["ln2_g"], wl["ln2_b"])


def _rope_tables(n_tokens):
    t = jnp.arange(n_tokens)
    row = (t // GRID_W).astype(F32)
    col = (t % GRID_W).astype(F32)
    n_freq = MLA_ROPE_DIM // 4
    inv_freq = ROPE_BASE ** (-jnp.arange(n_freq, dtype=F32) / n_freq)
    ar, ac = row[:, None] * inv_freq, col[:, None] * inv_freq
    z8 = jnp.zeros((n_tokens, n_freq), F32)
    ones = lambda w: jnp.ones((n_tokens, w), F32)
    zeros = lambda w: jnp.zeros((n_tokens, w), F32)
    cos_r, sin_r, cos_c, sin_c = jnp.cos(ar), jnp.sin(ar), jnp.cos(ac), jnp.sin(ac)
    tail = HEAD_LANES - MLA_NOPE_DIM - MLA_ROPE_DIM
    rc = jnp.concatenate([ones(MLA_NOPE_DIM), cos_r, cos_r, cos_c, cos_c, ones(tail)], axis=1)
    rsa = jnp.concatenate([zeros(MLA_NOPE_DIM), -sin_r, z8, -sin_c, z8, zeros(tail)], axis=1)
    rsb = jnp.concatenate([zeros(MLA_NOPE_DIM), z8, sin_r, z8, sin_c, zeros(tail)], axis=1)
    return rc, rsa, rsb


def _na_bias_table(rpb_l):
    col = jnp.arange(GRID_W)
    c_start = jnp.clip(col - NA_WIN_C // 2, 0, GRID_W - NA_WIN_C)
    col_mask = (col[None, :] >= c_start[:, None]) & (col[None, :] < c_start[:, None] + NA_WIN_C)
    dc_idx = jnp.clip(col[None, :] - col[:, None], -(NA_WIN_C - 1), NA_WIN_C - 1) + NA_WIN_C - 1
    blocks = jnp.where(col_mask[None, None], rpb_l[:, :, dc_idx], NEG_BIG)
    blocks = jnp.pad(blocks, ((0, 0), (NA_TAB_PAD, NA_TAB_PAD), (0, 0), (0, 0)), constant_values=NEG_BIG)
    pairs = jnp.concatenate([blocks[:, :-1], blocks[:, 1:]], axis=-1)
    return pairs.reshape(NA_HEADS * NA_TAB, GRID_W, 2 * GRID_W).astype(F32)


def _prep_layer(l, w_in, g_q, w_uq, g_kv, w_ukv, rpb, w_dw, b_dw, g_cn, b_cn, w_pw2, w_oa, w_ob, w_out,
                ln1_g, ln1_b, w_router, ln2_g, ln2_b):
    d = D_MODEL
    wi = w_in[l]
    off = [0, 512, 1024, 1536, 1792, 1920, 1952, 2976, 6048]
    zpad = lambda w: jnp.zeros((d, w), F32)
    w_main = jnp.concatenate([
        wi[:, off[0]:off[5]],
        zpad(MLA_NOPE_DIM), wi[:, off[5]:off[6]], zpad(HEAD_LANES - MLA_NOPE_DIM - MLA_ROPE_DIM),
        wi[:, off[6]:off[8]],
    ], axis=1).astype(BF16)
    hq = MLA_NOPE_DIM + MLA_ROPE_DIM
    wq = w_uq[l].reshape(MLA_Q_RANK, MLA_HEADS, hq)
    wq = jnp.pad(wq, ((0, 0), (0, 0), (0, HEAD_LANES - hq))).reshape(MLA_Q_RANK, MLA_HEADS * HEAD_LANES)
    wkv = w_ukv[l].reshape(MLA_KV_RANK, MLA_HEADS, MLA_NOPE_DIM + MLA_V_DIM)
    wk = jnp.pad(wkv[:, :, :MLA_NOPE_DIM], ((0, 0), (0, 0), (0, HEAD_LANES - MLA_NOPE_DIM)))
    wv = wkv[:, :, MLA_NOPE_DIM:]
    return dict(
        w_in=w_main, g_q=g_q[l][None], g_kv=g_kv[l][None],
        w_uq=wq.astype(BF16), w_uk=wk.reshape(MLA_KV_RANK, MLA_HEADS * HEAD_LANES).astype(BF16),
        w_uv=wv.reshape(MLA_KV_RANK, MLA_HEADS * MLA_V_DIM).astype(BF16),
        na_tab=_na_bias_table(rpb[l]),
        w_dw=w_dw[l], b_dw=b_dw[l][None], g_cn=g_cn[l][None], b_cn=b_cn[l][None],
        w_pw2=w_pw2[l].astype(BF16), w_oa=w_oa[l].astype(BF16), w_ob=w_ob[l].astype(BF16), w_out=w_out[l].astype(BF16),
        ln1_g=ln1_g[l][None], ln1_b=ln1_b[l][None], w_rt=w_router[l].T.astype(BF16),
        ln2_g=ln2_g[l][None], ln2_b=ln2_b[l][None])


def _moe(x1, h2, aff_t, mod_g2, mod_row, wl, w_gate_l, w_up_l, w_down_l, nb):
    n = x1.shape[1]
    cap = EC_CAPACITY_FACTOR * n // N_EXPERTS
    slot, slot_t = _route(aff_t, cap)
    y = _experts(slot, aff_t, h2, w_gate_l, w_up_l, w_down_l, cap, nb)
    return _combine(slot_t, y, x1, mod_g2, mod_row, wl, cap)


def kernel(x, c, ctx, c_ctx, w_ada, b_ada, w_in, g_q, w_uq, g_kv, w_ukv, rpb, w_dw, b_dw, g_cn, b_cn, w_pw2, w_oa,
           w_ob, w_out, ln1_g, ln1_b, w_router, w_gate, w_up, w_down, ln2_g, ln2_b):
    bsz, s, d = x.shape
    depth = w_ada.shape[0]
    ctx_row = bsz
    n_rows = -(-(bsz + 1) // SUBLANES) * SUBLANES
    cc = jnp.concatenate([c, c_ctx[None], jnp.zeros((n_rows - bsz - 1, d), F32)], axis=0)
    mod_all = _ada_all_layers(cc, w_ada, b_ada)
    rope = _rope_tables(s)
    for l in range(depth):
        last = l == depth - 1
        wl = _prep_layer(l, w_in, g_q, w_uq, g_kv, w_ukv, rpb, w_dw, b_dw, g_cn, b_cn, w_pw2, w_oa, w_ob, w_out,
                         ln1_g, ln1_b, w_router, ln2_g, ln2_b)
        mod = mod_all[l][:, None, :]
        mod_a, mod_e, mod_h = mod[:, :, 0:2 * d], mod[:, :, 2 * d:5 * d], mod[:, :, 5 * d:6 * d]
        qa, ka, va, qb, kb, vb, u, gl = _in_proj(x, mod_a, None, wl, rope, True)
        if last:
            ka_c, va_c, kb_c, vb_c = _in_proj(ctx, mod_a, ctx_row, wl, None, False)
        else:
            qa_c, ka_c, va_c, qb_c, kb_c, vb_c, u_c, gl_c = _in_proj(ctx, mod_a, ctx_row, wl, None, True)
        ya = _na_attention(qa, ka, va, ka_c, va_c, wl["na_tab"])
        yb = _dense_attention(qb, [(kb_c, vb_c), (kb, vb)], False)
        z = _conv_module(u, wl)
        x1, h2, aff_t = _merge(x, ya, yb, z, gl, mod_e, None, wl)
        x = _moe(x1, h2, aff_t, mod_h, None, wl, w_gate[l], w_up[l], w_down[l], 1)
        if not last:
            ya_c = _dense_attention(qa_c, [(ka_c, va_c)], True)
            yb_c = _dense_attention(qb_c, [(kb_c, vb_c)], False)
            z_c = _conv_module(u_c, wl)
            c1, h2_c, aff_c = _merge(ctx, ya_c, yb_c, z_c, gl_c, mod_e, ctx_row, wl)
            ctx = _moe(c1, h2_c, aff_c, mod_h, ctx_row, wl, w_gate[l], w_up[l], w_down[l], bsz)
    return x
```

```python
import functools

import jax
import jax.numpy as jnp
from jax import lax
from jax.experimental import pallas as pl
from jax.experimental.pallas import tpu as pltpu

F32 = jnp.float32
BF16 = jnp.bfloat16

D_MODEL = 1024
DEPTH = 4
GRID_W = 64
NA_HEADS = 8
NA_HEAD_DIM = 64
NA_WIN_R = 8
NA_WIN_C = 16
NA_WIDTH = NA_HEADS * NA_HEAD_DIM
MLA_HEADS = 8
MLA_Q_RANK = 256
MLA_KV_RANK = 128
MLA_NOPE_DIM = 64
MLA_ROPE_DIM = 32
MLA_V_DIM = 64
ROPE_BASE = 10000.0
CONV_DIM = 512
CONV_WIDTH = 31
N_EXPERTS = 16
EXPERT_DIM = 1024
EC_CAPACITY_FACTOR = 2
LN_EPS = 1e-5
RMS_EPS = 1e-6
DEEPNORM_ALPHA = (2 * DEPTH) ** 0.25

LANES = 128
SUBLANES = 8
VMEM_LIMIT_BYTES = 56 * 1024 * 1024

ROW_TILE = 256
NA_Q_ROWS = 4
NA_BAND_ROWS = NA_Q_ROWS + NA_WIN_R
CONV_CHUNK = 64
CONV_HALO = 16
HEAD_LANES = 128
MLA_KEY_CHUNK = 4096
MLA_Q_TILE = 512
NEG_BIG = -1e30
LOG2_E = 1.4426950408889634

C_QA, C_KA, C_VA = 0, 512, 1024
C_CQ, C_CKV, C_KR = 1536, 1792, 1920
C_CONV_A, C_CONV_G, C_GATES = 2048, 2560, 3072
IN_COLS_PAD = 6144
NA_TAB_PAD = 5
NA_TAB = 2 * NA_WIN_R - 1 + 2 * NA_TAB_PAD - 1


def _params(*sem):
    return pltpu.CompilerParams(dimension_semantics=sem, vmem_limit_bytes=VMEM_LIMIT_BYTES)


def _dot(a, b):
    return jnp.dot(a, b, preferred_element_type=F32)


def _dot_nt(a, b):
    return lax.dot_general(a, b, (((1,), (1,)), ((), ())), preferred_element_type=F32)


def _layer_norm(x, g, b):
    mu = jnp.mean(x, axis=-1, keepdims=True)
    var = jnp.mean(jnp.square(x - mu), axis=-1, keepdims=True)
    return (x - mu) * lax.rsqrt(var + LN_EPS) * g + b


def _rms_norm(x, g):
    return x * lax.rsqrt(jnp.mean(jnp.square(x), axis=-1, keepdims=True) + RMS_EPS) * g


def _silu(x):
    return x * jax.nn.sigmoid(x)


def _ada_kernel(c_ref, w_ref, b_ref, o_ref):
    s = _silu(c_ref[...]).astype(BF16)
    o_ref[0] = _dot(s, w_ref[0].astype(BF16)) + b_ref[0]


def _ada_all_layers(cc, w_ada, b_ada):
    n_layers, d, cols = w_ada.shape
    rows = cc.shape[0]
    tn = 1536
    return pl.pallas_call(
        _ada_kernel,
        grid=(n_layers, cols // tn),
        in_specs=[pl.BlockSpec((rows, d), lambda l, j: (0, 0)),
                  pl.BlockSpec((1, d, tn), lambda l, j: (l, 0, j)),
                  pl.BlockSpec((1, 1, tn), lambda l, j: (l, 0, j))],
        out_specs=pl.BlockSpec((1, rows, tn), lambda l, j: (l, 0, j)),
        out_shape=jax.ShapeDtypeStruct((n_layers, rows, cols), F32),
        compiler_params=_params("arbitrary", "arbitrary"),
        name="ada_ln",
    )(cc, w_ada, b_ada.reshape(n_layers, 1, cols))


def _rope(x, c, sa, sb):
    return x * c + pltpu.roll(x, HEAD_LANES - 8, axis=1) * sa + pltpu.roll(x, 8, axis=1) * sb


def _in_proj_kernel(*refs, use_rope, full):
    if use_rope:
        (x_ref, mod_ref, w_ref, gq_ref, gkv_ref, wuq_ref, wuk_ref, wuv_ref, rc_ref, rsa_ref, rsb_ref), outs = refs[:11], refs[11:]
    else:
        (x_ref, mod_ref, w_ref, gq_ref, gkv_ref, wuq_ref, wuk_ref, wuv_ref), outs = refs[:8], refs[8:]
    if full:
        qa_ref, ka_ref, va_ref, qb_ref, kb_ref, vb_ref, u_ref, gl_ref = outs
    else:
        ka_ref, va_ref, kb_ref, vb_ref = outs
    d = D_MODEL
    mod = mod_ref[0]
    h = (x_ref[0] * (1.0 + mod[:, d:2 * d]) + mod[:, 0:d]).astype(BF16)

    def proj(c0, width):
        return _dot(h, w_ref[:, c0:c0 + width])

    ka_ref[0] = proj(C_KA, NA_WIDTH).astype(BF16)
    va_ref[0] = proj(C_VA, NA_WIDTH).astype(BF16)
    if use_rope:
        rc, rsa, rsb = rc_ref[...], rsa_ref[...], rsb_ref[...]
    n_kv = _rms_norm(proj(C_CKV, MLA_KV_RANK), gkv_ref[...]).astype(BF16)
    kr = proj(C_KR, HEAD_LANES)
    if use_rope:
        kr = _rope(kr, rc, rsa, rsb)
    k_nope = _dot(n_kv, wuk_ref[...])
    for hd in range(MLA_HEADS):
        sl = slice(hd * HEAD_LANES, (hd + 1) * HEAD_LANES)
        kb_ref[0, :, sl] = (k_nope[:, sl] + kr).astype(BF16)
    v_t = _dot(n_kv, wuv_ref[...]).T
    ones = jnp.ones((MLA_V_DIM, v_t.shape[1]), BF16)
    for hd in range(MLA_HEADS):
        vb_ref[0, hd * HEAD_LANES:hd * HEAD_LANES + MLA_V_DIM, :] = v_t[hd * MLA_V_DIM:(hd + 1) * MLA_V_DIM, :].astype(BF16)
        vb_ref[0, hd * HEAD_LANES + MLA_V_DIM:(hd + 1) * HEAD_LANES, :] = ones
    if not full:
        return
    qa_ref[0] = (proj(C_QA, NA_WIDTH) * (LOG2_E * NA_HEAD_DIM ** -0.5)).astype(BF16)
    n_q = _rms_norm(proj(C_CQ, MLA_Q_RANK), gq_ref[...]).astype(BF16)
    q = _dot(n_q, wuq_ref[...])
    mla_scale = LOG2_E * (MLA_NOPE_DIM + MLA_ROPE_DIM) ** -0.5
    for hd in range(MLA_HEADS):
        sl = slice(hd * HEAD_LANES, (hd + 1) * HEAD_LANES)
        qh = q[:, sl]
        if use_rope:
            qh = _rope(qh, rc, rsa, rsb)
        qb_ref[0, :, sl] = (qh * mla_scale).astype(BF16)
    u_ref[0] = proj(C_CONV_A, CONV_DIM) * jax.nn.sigmoid(proj(C_CONV_G, CONV_DIM))
    gl_ref[0] = proj(C_GATES, 3 * d)


def _in_proj(x, mod, mod_row, wl, rope, full):
    bsz, n, d = x.shape
    tm = ROW_TILE
    use_rope = rope is not None
    if mod_row is None:
        mod_map = lambda b, i: (b, 0, 0)
    else:
        mod_map = lambda b, i: (mod_row, 0, 0)
    const2 = lambda b, i: (0, 0)
    row_spec = lambda w: pl.BlockSpec((1, tm, w), lambda b, i: (b, i, 0))
    in_specs = [row_spec(d),
                pl.BlockSpec((1, 1, 2 * d), mod_map),
                pl.BlockSpec((d, IN_COLS_PAD), const2, pipeline_mode=pl.Buffered(1)),
                pl.BlockSpec((1, MLA_Q_RANK), const2),
                pl.BlockSpec((1, MLA_KV_RANK), const2),
                pl.BlockSpec((MLA_Q_RANK, MLA_HEADS * HEAD_LANES), const2),
                pl.BlockSpec((MLA_KV_RANK, MLA_HEADS * HEAD_LANES), const2),
                pl.BlockSpec((MLA_KV_RANK, MLA_HEADS * MLA_V_DIM), const2)]
    args = [x, mod, wl["w_in"], wl["g_q"], wl["g_kv"], wl["w_uq"], wl["w_uk"], wl["w_uv"]]
    if use_rope:
        in_specs += [pl.BlockSpec((tm, HEAD_LANES), lambda b, i: (i, 0))] * 3
        args += list(rope)
    sd = lambda w, dt: jax.ShapeDtypeStruct((bsz, n, w), dt)
    kv_shapes = [sd(NA_WIDTH, BF16), sd(NA_WIDTH, BF16), sd(MLA_HEADS * HEAD_LANES, BF16),
                 jax.ShapeDtypeStruct((bsz, MLA_HEADS * HEAD_LANES, n), BF16)]
    kv_specs = [row_spec(NA_WIDTH), row_spec(NA_WIDTH), row_spec(MLA_HEADS * HEAD_LANES),
                pl.BlockSpec((1, MLA_HEADS * HEAD_LANES, tm), lambda b, i: (b, 0, i))]
    if full:
        out_shape = [sd(NA_WIDTH, BF16), kv_shapes[0], kv_shapes[1], sd(MLA_HEADS * HEAD_LANES, BF16), kv_shapes[2],
                     kv_shapes[3], sd(CONV_DIM, F32), sd(3 * d, F32)]
        out_specs = [row_spec(NA_WIDTH), kv_specs[0], kv_specs[1], row_spec(MLA_HEADS * HEAD_LANES), kv_specs[2],
                     kv_specs[3], row_spec(CONV_DIM), row_spec(3 * d)]
    else:
        out_shape, out_specs = kv_shapes, kv_specs
    return pl.pallas_call(
        functools.partial(_in_proj_kernel, use_rope=use_rope, full=full),
        grid=(bsz, n // tm),
        in_specs=in_specs,
        out_specs=out_specs,
        out_shape=out_shape,
        compiler_params=_params("parallel", "arbitrary"),
        name="in_proj",
    )(*args)


def _pair_lane_mask(rows):
    return lax.broadcasted_iota(jnp.int32, (rows, LANES), 1) < NA_HEAD_DIM


def _ones_in_other_half(v_pair, hh):
    keep = _pair_lane_mask(v_pair.shape[0])
    if hh == 1:
        keep = jnp.logical_not(keep)
    return jnp.where(keep, v_pair, jnp.ones_like(v_pair))


def _normalise_by_other_half(o):
    return o / pltpu.roll(o, NA_HEAD_DIM, axis=1)


def _na_kernel(q_ref, k_ref, v_ref, kc_ref, vc_ref, tab_ref, o_ref, *, grid_rows):
    i = pl.program_id(1)
    tq = NA_Q_ROWS * GRID_W
    nband = NA_BAND_ROWS * GRID_W
    r0 = jnp.clip(NA_Q_ROWS * i - NA_WIN_R // 2, 0, grid_rows - NA_BAND_ROWS)
    kstart = pl.multiple_of(r0 * GRID_W, GRID_W)
    qrow = NA_Q_ROWS * i + lax.broadcasted_iota(jnp.int32, (tq, nband), 0) // GRID_W
    krow = r0 + lax.broadcasted_iota(jnp.int32, (tq, nband), 1) // GRID_W
    rstart = jnp.clip(qrow - NA_WIN_R // 2, 0, grid_rows - NA_WIN_R)
    valid = (krow >= rstart) & (krow < rstart + NA_WIN_R)
    tab_off = r0 - NA_Q_ROWS * i + (NA_WIN_R - 1) + NA_TAB_PAD
    first = _pair_lane_mask(tq)
    for p in range(NA_HEADS // 2):
        ls = slice(p * LANES, (p + 1) * LANES)
        qp = q_ref[0, :, ls]
        kband = k_ref[0, pl.ds(kstart, nband), ls]
        vband = v_ref[0, pl.ds(kstart, nband), ls]
        kctx = kc_ref[0, :, ls]
        vctx = vc_ref[0, :, ls]
        halves = []
        for hh in range(2):
            hd = 2 * p + hh
            qm = jnp.where(first if hh == 0 else jnp.logical_not(first), qp, jnp.zeros_like(qp))
            sw = _dot_nt(qm, kband)
            sc = _dot_nt(qm, kctx)
            bias_rows = []
            for a in range(NA_Q_ROWS):
                blocks = [tab_ref[hd * NA_TAB + tab_off + 2 * m - a] for m in range(NA_BAND_ROWS // 2)]
                bias_rows.append(jnp.concatenate(blocks, axis=1))
            sw = jnp.where(valid, sw + jnp.concatenate(bias_rows, axis=0), NEG_BIG)
            mx = jnp.maximum(jnp.max(sw, axis=1, keepdims=True), jnp.max(sc, axis=1, keepdims=True))
            pw = jnp.exp2(sw - mx)
            pc = jnp.exp2(sc - mx)
            o = (_dot(pw.astype(BF16), _ones_in_other_half(vband, hh))
                 + _dot(pc.astype(BF16), _ones_in_other_half(vctx, hh)))
            halves.append(_normalise_by_other_half(o))
        o_ref[0, :, ls] = jnp.where(first, halves[0], halves[1]).astype(BF16)


def _na_attention(qa, ka, va, ka_c, va_c, tab):
    bsz, s, w = qa.shape
    n_ctx = ka_c.shape[1]
    grid_rows = s // GRID_W
    assert grid_rows >= NA_BAND_ROWS and grid_rows % NA_Q_ROWS == 0
    tq = NA_Q_ROWS * GRID_W
    return pl.pallas_call(
        functools.partial(_na_kernel, grid_rows=grid_rows),
        grid=(bsz, s // tq),
        in_specs=[pl.BlockSpec((1, tq, w), lambda b, i: (b, i, 0)),
                  pl.BlockSpec((1, s, w), lambda b, i: (b, 0, 0)),
                  pl.BlockSpec((1, s, w), lambda b, i: (b, 0, 0)),
                  pl.BlockSpec((1, n_ctx, w), lambda b, i: (b, 0, 0)),
                  pl.BlockSpec((1, n_ctx, w), lambda b, i: (b, 0, 0)),
                  pl.BlockSpec(tab.shape, lambda b, i: (0, 0, 0), pipeline_mode=pl.Buffered(1))],
        out_specs=pl.BlockSpec((1, tq, w), lambda b, i: (b, i, 0)),
        out_shape=jax.ShapeDtypeStruct((bsz, s, w), BF16),
        compiler_params=_params("parallel", "arbitrary"),
        name="na_attn",
    )(qa, ka, va, ka_c, va_c, tab)


def _dense_attn_kernel(*refs, n_seg, paired_q):
    q_ref, kv_refs, o_ref = refs[0], refs[1:1 + 2 * n_seg], refs[1 + 2 * n_seg]
    tq = q_ref.shape[1]
    first = _pair_lane_mask(tq)
    n_heads = o_ref.shape[2] // NA_HEAD_DIM
    for p in range(n_heads // 2):
        ls = slice(p * LANES, (p + 1) * LANES)
        halves = []
        for hh in range(2):
            hd = 2 * p + hh
            if paired_q:
                qp = q_ref[0, :, ls]
                qh = jnp.where(first if hh == 0 else jnp.logical_not(first), qp, jnp.zeros_like(qp))
                ks = ls
            else:
                ks = slice(hd * HEAD_LANES, (hd + 1) * HEAD_LANES)
                qh = q_ref[0, :, ks]
            scores = [_dot_nt(qh, kv_refs[2 * j][0, :, ks]) for j in range(n_seg)]
            mx = functools.reduce(jnp.maximum, [jnp.max(sj, axis=1, keepdims=True) for sj in scores])
            o = functools.reduce(lambda a, b: a + b,
                                 [_dot(jnp.exp2(scores[j] - mx).astype(BF16), _ones_in_other_half(kv_refs[2 * j + 1][0, :, ls], hh))
                                  for j in range(n_seg)])
            halves.append(_normalise_by_other_half(o))
        o_ref[0, :, ls] = jnp.where(first, halves[0], halves[1]).astype(BF16)


def _mla_attn_kernel(*refs, n_seg):
    q_ref, kv_refs, o_ref = refs[0], refs[1:1 + 2 * n_seg], refs[1 + 2 * n_seg]
    chunks = []
    for j in range(n_seg):
        nk = kv_refs[2 * j].shape[1]
        step = nk if nk <= MLA_KEY_CHUNK else MLA_KEY_CHUNK
        chunks += [(j, c0, step) for c0 in range(0, nk, step)]

    def scores_of(hd):
        hs = slice(hd * HEAD_LANES, (hd + 1) * HEAD_LANES)
        qh = q_ref[0, :, hs]
        return [_dot_nt(kv_refs[2 * j][0, c0:c0 + sz, hs], qh) for j, c0, sz in chunks]

    def finish(hd, scores):
        hs = slice(hd * HEAD_LANES, (hd + 1) * HEAD_LANES)
        mx = functools.reduce(jnp.maximum, [jnp.max(sj, axis=0, keepdims=True) for sj in scores])
        o_t = functools.reduce(lambda a, b: a + b,
                               [_dot(kv_refs[2 * j + 1][0, hs, c0:c0 + sz], jnp.exp2(sj - mx).astype(BF16))
                                for (j, c0, sz), sj in zip(chunks, scores)])
        return o_t[0:MLA_V_DIM, :] / o_t[MLA_V_DIM:HEAD_LANES, :]

    outs = []
    nxt = scores_of(0)
    for hd in range(MLA_HEADS):
        cur = nxt
        if hd + 1 < MLA_HEADS:
            nxt = scores_of(hd + 1)
        outs.append(finish(hd, cur))
    for p in range(MLA_HEADS // 2):
        o_ref[0, :, p * LANES:(p + 1) * LANES] = jnp.concatenate(outs[2 * p:2 * p + 2], axis=0).T.astype(BF16)


def _mla_attention(q, kvs):
    bsz, nq, wq = q.shape
    tq = min(nq, MLA_Q_TILE)
    in_specs = [pl.BlockSpec((1, tq, wq), lambda b, i: (b, i, 0))]
    args = [q]
    for k, v_t in kvs:
        in_specs += [pl.BlockSpec((1, k.shape[1], k.shape[2]), lambda b, i: (b, 0, 0)),
                     pl.BlockSpec((1, v_t.shape[1], v_t.shape[2]), lambda b, i: (b, 0, 0))]
        args += [k, v_t]
    return pl.pallas_call(
        functools.partial(_mla_attn_kernel, n_seg=len(kvs)),
        grid=(bsz, nq // tq),
        in_specs=in_specs,
        out_specs=pl.BlockSpec((1, tq, MLA_HEADS * MLA_V_DIM), lambda b, i: (b, i, 0)),
        out_shape=jax.ShapeDtypeStruct((bsz, nq, MLA_HEADS * MLA_V_DIM), BF16),
        compiler_params=_params("parallel", "arbitrary"),
        name="mla_attn",
    )(*args)


def _dense_attention(q, kvs, paired_q):
    bsz, nq, wq = q.shape
    tq = ROW_TILE
    wv = kvs[0][1].shape[2]
    in_specs = [pl.BlockSpec((1, tq, wq), lambda b, i: (b, i, 0))]
    args = [q]
    for k, v in kvs:
        in_specs += [pl.BlockSpec((1, k.shape[1], k.shape[2]), lambda b, i: (b, 0, 0)),
                     pl.BlockSpec((1, v.shape[1], v.shape[2]), lambda b, i: (b, 0, 0))]
        args += [k, v]
    return pl.pallas_call(
        functools.partial(_dense_attn_kernel, n_seg=len(kvs), paired_q=paired_q),
        grid=(bsz, nq // tq),
        in_specs=in_specs,
        out_specs=pl.BlockSpec((1, tq, wv), lambda b, i: (b, i, 0)),
        out_shape=jax.ShapeDtypeStruct((bsz, nq, wv), BF16),
        compiler_params=_params("parallel", "arbitrary"),
        name="dense_attn",
    )(*args)


def _conv_kernel(u_ref, w_ref, b_ref, g_ref, bn_ref, z_ref, pad_ref):
    n = u_ref.shape[1]
    half = CONV_WIDTH // 2
    zeros = jnp.zeros((CONV_HALO, CONV_DIM), F32)
    pad_ref[0:CONV_HALO, :] = zeros
    pad_ref[CONV_HALO + n:2 * CONV_HALO + n, :] = zeros
    pad_ref[CONV_HALO:CONV_HALO + n, :] = u_ref[0]

    def chunk(j, carry):
        t0 = pl.multiple_of(j * CONV_CHUNK, CONV_CHUNK)
        rows = CONV_CHUNK + 2 * CONV_HALO
        accs = []
        for cb in range(CONV_DIM // LANES):
            ls = slice(cb * LANES, (cb + 1) * LANES)
            win = pad_ref[pl.ds(t0, rows), ls]
            a = jnp.zeros((CONV_CHUNK, LANES), F32)
            for r in range(SUBLANES):
                ph = win if r == 0 else pltpu.roll(win, rows - r, axis=0)
                for lo in range(r, CONV_HALO - half + CONV_WIDTH, SUBLANES):
                    k = lo - (CONV_HALO - half)
                    if k >= 0:
                        a = a + ph[lo - r:lo - r + CONV_CHUNK, :] * w_ref[k:k + 1, ls]
            accs.append(a)
        acc = jnp.concatenate(accs, axis=1)
        y = _layer_norm(acc + b_ref[...], g_ref[...], bn_ref[...])
        z_ref[0, pl.ds(t0, CONV_CHUNK), :] = _silu(y).astype(BF16)
        return carry

    lax.fori_loop(0, n // CONV_CHUNK, chunk, 0)


def _conv_module(u, wl):
    bsz, n, c = u.shape
    vec = pl.BlockSpec((1, c), lambda b: (0, 0))
    return pl.pallas_call(
        _conv_kernel,
        grid=(bsz,),
        in_specs=[pl.BlockSpec((1, n, c), lambda b: (b, 0, 0)),
                  pl.BlockSpec((CONV_WIDTH, c), lambda b: (0, 0)), vec, vec, vec],
        out_specs=pl.BlockSpec((1, n, c), lambda b: (b, 0, 0)),
        out_shape=jax.ShapeDtypeStruct((bsz, n, c), BF16),
        scratch_shapes=[pltpu.VMEM((n + 2 * CONV_HALO, c), F32)],
        compiler_params=_params("parallel"),
        name="conv_module",
    )(u, wl["w_dw"], wl["b_dw"], wl["g_cn"], wl["b_cn"])


def _merge_kernel(x_ref, ya_ref, yb_ref, z_ref, gl_ref, mod_ref, woa_ref, wob_ref, wpw_ref, wout_ref,
                  g_ref, b_ref, wrt_ref, x1_ref, h2_ref, aff_ref):
    d = D_MODEL
    gl = gl_ref[0]
    m = (jax.nn.sigmoid(gl[:, 0:d]) * _dot(ya_ref[0], woa_ref[...])
         + jax.nn.sigmoid(gl[:, d:2 * d]) * _dot(yb_ref[0], wob_ref[...])
         + jax.nn.sigmoid(gl[:, 2 * d:3 * d]) * _dot(z_ref[0], wpw_ref[...]))
    y = _dot(m.astype(BF16), wout_ref[...])
    mod = mod_ref[0]
    x1 = _layer_norm(DEEPNORM_ALPHA * x_ref[0] + mod[:, 0:d] * y, g_ref[...], b_ref[...])
    x1_ref[0] = x1
    h2 = (x1 * (1.0 + mod[:, 2 * d:3 * d]) + mod[:, d:2 * d]).astype(BF16)
    h2_ref[0] = h2
    logits = _dot_nt(wrt_ref[...], h2)
    e = jnp.exp(logits - jnp.max(logits, axis=0, keepdims=True))
    aff_ref[0] = e / jnp.sum(e, axis=0, keepdims=True)


def _merge(x, ya, yb, z, gl, mod, mod_row, wl):
    bsz, n, d = x.shape
    tm = ROW_TILE
    if mod_row is None:
        mod_map = lambda b, i: (b, 0, 0)
    else:
        mod_map = lambda b, i: (mod_row, 0, 0)
    row = lambda w: pl.BlockSpec((1, tm, w), lambda b, i: (b, i, 0))
    const2 = lambda b, i: (0, 0)
    wspec = lambda r: pl.BlockSpec((r, d), const2)
    return pl.pallas_call(
        _merge_kernel,
        grid=(bsz, n // tm),
        in_specs=[row(d), row(NA_WIDTH), row(NA_WIDTH), row(CONV_DIM), row(3 * d),
                  pl.BlockSpec((1, 1, 3 * d), mod_map),
                  wspec(NA_WIDTH), wspec(NA_WIDTH), wspec(CONV_DIM), wspec(d),
                  pl.BlockSpec((1, d), const2), pl.BlockSpec((1, d), const2),
                  pl.BlockSpec((N_EXPERTS, d), const2)],
        out_specs=[row(d), row(d), pl.BlockSpec((1, N_EXPERTS, tm), lambda b, i: (b, 0, i))],
        out_shape=[jax.ShapeDtypeStruct((bsz, n, d), F32), jax.ShapeDtypeStruct((bsz, n, d), BF16),
                   jax.ShapeDtypeStruct((bsz, N_EXPERTS, n), F32)],
        compiler_params=_params("parallel", "arbitrary"),
        name="merge_ln1_router",
    )(x, ya, yb, z, gl, mod, wl["w_oa"], wl["w_ob"], wl["w_pw2"], wl["w_out"], wl["ln1_g"], wl["ln1_b"], wl["w_rt"])


def _lane_cumsum(v):
    n = v.shape[1]
    lane = lax.broadcasted_iota(jnp.int32, v.shape, 1)
    s = 1
    while s < n:
        v = v + jnp.where(lane >= s, pltpu.roll(v, s, axis=1), 0)
        s *= 2
    return v


def _route_kernel(aff_ref, slot_ref, slot_t_ref, *, cap):
    bits = pltpu.bitcast(aff_ref[0], jnp.int32)
    e, n = bits.shape

    def search(it, thr):
        cand = thr | lax.shift_left(jnp.int32(1), 30 - it)
        cnt = jnp.sum(jnp.where(bits >= cand, 1.0, 0.0), axis=1, keepdims=True)
        return jnp.where(cnt >= cap, cand, thr)

    thr = lax.fori_loop(0, 31, search, jnp.zeros((e, 1), jnp.int32))
    gt = jnp.where(bits > thr, 1, 0)
    eq = jnp.where(bits == thr, 1, 0)
    inc = _lane_cumsum(gt + eq * 65536)
    gt_before = (inc & 65535) - gt
    eq_before = lax.shift_right_logical(inc, 16) - eq
    need = cap - jnp.sum(gt, axis=1, keepdims=True)
    sel = (gt == 1) | ((eq == 1) & (eq_before < need))
    slot = jnp.where(sel, gt_before + jnp.minimum(eq_before, need), -1)
    slot_ref[0] = slot
    padded = jnp.concatenate([slot.astype(F32), jnp.full((LANES - e, n), -1.0, F32)], axis=0)
    slot_t_ref[0] = padded.T.astype(jnp.int32)


def _route(aff_t, cap):
    bsz, e, n = aff_t.shape
    return pl.pallas_call(
        functools.partial(_route_kernel, cap=cap),
        grid=(bsz,),
        in_specs=[pl.BlockSpec((1, e, n), lambda b: (b, 0, 0))],
        out_specs=[pl.BlockSpec((1, e, n), lambda b: (b, 0, 0)), pl.BlockSpec((1, n, LANES), lambda b: (b, 0, 0))],
        out_shape=[jax.ShapeDtypeStruct((bsz, e, n), jnp.int32), jax.ShapeDtypeStruct((bsz, n, LANES), jnp.int32)],
        compiler_params=_params("parallel"),
        name="route",
    )(aff_t)


def _expert_kernel(slot_ref, aff_ref, h_ref, wg_ref, wu_ref, wd_ref, y_ref, wg_s, wu_s, wd_s, xg_s, ac_s, *, cap):
    @pl.when(pl.program_id(1) == 0)
    def _():
        wg_s[...] = wg_ref[0, 0].astype(BF16)
        wu_s[...] = wu_ref[0, 0].astype(BF16)
        wd_s[...] = wd_ref[0, 0].astype(BF16)

    nb, n = h_ref.shape[0], h_ref.shape[1]
    c_iota = lax.broadcasted_iota(jnp.int32, (cap, n), 0)
    for j in range(nb):
        hit = slot_ref[j, 0] == c_iota
        onehot = jnp.where(hit, 1.0, 0.0).astype(BF16)
        xg_s[j * cap:(j + 1) * cap, :] = _dot(onehot, h_ref[j]).astype(BF16)
        ac_s[j * cap:(j + 1) * cap, :] = jnp.sum(jnp.where(hit, aff_ref[j, 0], 0.0), axis=1, keepdims=True)
    xg = xg_s[...]
    hid = _silu(_dot(xg, wg_s[...])) * _dot(xg, wu_s[...])
    y = _dot(hid.astype(BF16), wd_s[...]) * ac_s[...]
    for j in range(nb):
        y_ref[j, 0] = y[j * cap:(j + 1) * cap, :].astype(BF16)


def _experts(slot, aff_t, h2, w_gate, w_up, w_down, layer, cap, nb):
    bsz, e, n = slot.shape
    d, f = w_gate.shape[2], w_gate.shape[3]
    slot4 = slot.reshape(bsz, e, 1, n)
    aff4 = aff_t.reshape(bsz, e, 1, n)
    return pl.pallas_call(
        functools.partial(_expert_kernel, cap=cap),
        grid=(e, bsz // nb),
        in_specs=[pl.BlockSpec((nb, 1, 1, n), lambda ei, bi: (bi, ei, 0, 0)),
                  pl.BlockSpec((nb, 1, 1, n), lambda ei, bi: (bi, ei, 0, 0)),
                  pl.BlockSpec((nb, n, d), lambda ei, bi: (bi, 0, 0)),
                  pl.BlockSpec((1, 1, d, f), lambda ei, bi: (layer, ei, 0, 0)),
                  pl.BlockSpec((1, 1, d, f), lambda ei, bi: (layer, ei, 0, 0)),
                  pl.BlockSpec((1, 1, f, d), lambda ei, bi: (layer, ei, 0, 0))],
        out_specs=pl.BlockSpec((nb, 1, cap, d), lambda ei, bi: (bi, ei, 0, 0)),
        out_shape=jax.ShapeDtypeStruct((bsz, e, cap, d), BF16),
        scratch_shapes=[pltpu.VMEM((d, f), BF16), pltpu.VMEM((d, f), BF16), pltpu.VMEM((f, d), BF16),
                        pltpu.VMEM((nb * cap, d), BF16), pltpu.VMEM((nb * cap, 1), F32)],
        compiler_params=_params("arbitrary", "arbitrary"),
        name="experts",
    )(slot4, aff4, h2, w_gate, w_up, w_down)


def _combine_kernel(st_ref, y_ref, x_ref, mod_ref, g_ref, b_ref, o_ref, *, cap):
    st = st_ref[0]
    rows = st.shape[0]
    if cap % LANES == 0:
        c_iota = lax.broadcasted_iota(jnp.int32, (rows, cap), 1)
        scat = jnp.concatenate(
            [jnp.where(st[:, e:e + 1] == c_iota, 1.0, 0.0).astype(BF16) for e in range(N_EXPERTS)], axis=1)
    else:
        j_iota = lax.broadcasted_iota(jnp.int32, (rows, N_EXPERTS * cap), 1)
        hit = None
        for e in range(N_EXPERTS):
            col = st[:, e:e + 1]
            he = jnp.where(col >= 0, col + e * cap, -1) == j_iota
            hit = he if hit is None else (hit | he)
        scat = jnp.where(hit, 1.0, 0.0).astype(BF16)
    moe = _dot(scat, y_ref[0])
    o_ref[0] = _layer_norm(DEEPNORM_ALPHA * x_ref[0] + mod_ref[0] * moe, g_ref[...], b_ref[...])


def _combine(slot_t, y, x1, mod, mod_row, wl, cap):
    bsz, n, d = x1.shape
    tm = ROW_TILE
    if mod_row is None:
        mod_map = lambda b, i: (b, 0, 0)
    else:
        mod_map = lambda b, i: (mod_row, 0, 0)
    ec = N_EXPERTS * cap
    return pl.pallas_call(
        functools.partial(_combine_kernel, cap=cap),
        grid=(bsz, n // tm),
        in_specs=[pl.BlockSpec((1, tm, LANES), lambda b, i: (b, i, 0)),
                  pl.BlockSpec((1, ec, d), lambda b, i: (b, 0, 0)),
                  pl.BlockSpec((1, tm, d), lambda b, i: (b, i, 0)),
                  pl.BlockSpec((1, 1, d), mod_map),
                  pl.BlockSpec((1, d), lambda b, i: (0, 0)), pl.BlockSpec((1, d), lambda b, i: (0, 0))],
        out_specs=pl.BlockSpec((1, tm, d), lambda b, i: (b, i, 0)),
        out_shape=jax.ShapeDtypeStruct((bsz, n, d), F32),
        compiler_params=_params("parallel", "arbitrary"),
        name="combine_ln2",
    )(slot_t, y.reshape(bsz, ec, d), x1, mod, wl["ln2_g"], wl["ln2_b"])


def _rope_tables(n_tokens):
    t = jnp.arange(n_tokens)
    row = (t // GRID_W).astype(F32)
    col = (t % GRID_W).astype(F32)
    n_freq = MLA_ROPE_DIM // 4
    inv_freq = ROPE_BASE ** (-jnp.arange(n_freq, dtype=F32) / n_freq)
    ar, ac = row[:, None] * inv_freq, col[:, None] * inv_freq
    z8 = jnp.zeros((n_tokens, n_freq), F32)
    ones = lambda w: jnp.ones((n_tokens, w), F32)
    zeros = lambda w: jnp.zeros((n_tokens, w), F32)
    cos_r, sin_r, cos_c, sin_c = jnp.cos(ar), jnp.sin(ar), jnp.cos(ac), jnp.sin(ac)
    tail = HEAD_LANES - MLA_NOPE_DIM - MLA_ROPE_DIM
    rc = jnp.concatenate([ones(MLA_NOPE_DIM), cos_r, cos_r, cos_c, cos_c, ones(tail)], axis=1)
    rsa = jnp.concatenate([zeros(MLA_NOPE_DIM), -sin_r, z8, -sin_c, z8, zeros(tail)], axis=1)
    rsb = jnp.concatenate([zeros(MLA_NOPE_DIM), z8, sin_r, z8, sin_c, zeros(tail)], axis=1)
    return rc, rsa, rsb


def _na_bias_table(rpb_l):
    col = jnp.arange(GRID_W)
    c_start = jnp.clip(col - NA_WIN_C // 2, 0, GRID_W - NA_WIN_C)
    col_mask = (col[None, :] >= c_start[:, None]) & (col[None, :] < c_start[:, None] + NA_WIN_C)
    dc_idx = jnp.clip(col[None, :] - col[:, None], -(NA_WIN_C - 1), NA_WIN_C - 1) + NA_WIN_C - 1
    blocks = jnp.where(col_mask[None, None], rpb_l[:, :, dc_idx] * LOG2_E, NEG_BIG)
    blocks = jnp.pad(blocks, ((0, 0), (NA_TAB_PAD, NA_TAB_PAD), (0, 0), (0, 0)), constant_values=NEG_BIG)
    pairs = jnp.concatenate([blocks[:, :-1], blocks[:, 1:]], axis=-1)
    return pairs.reshape(NA_HEADS * NA_TAB, GRID_W, 2 * GRID_W).astype(F32)


def _prep_layer(l, w_in, g_q, w_uq, g_kv, w_ukv, rpb, w_dw, b_dw, g_cn, b_cn, w_pw2, w_oa, w_ob, w_out,
                ln1_g, ln1_b, w_router, ln2_g, ln2_b):
    d = D_MODEL
    wi = w_in[l]
    off = [0, 512, 1024, 1536, 1792, 1920, 1952, 2976, 6048]
    zpad = lambda w: jnp.zeros((d, w), F32)
    w_main = jnp.concatenate([
        wi[:, off[0]:off[5]],
        zpad(MLA_NOPE_DIM), wi[:, off[5]:off[6]], zpad(HEAD_LANES - MLA_NOPE_DIM - MLA_ROPE_DIM),
        wi[:, off[6]:off[8]],
    ], axis=1).astype(BF16)
    hq = MLA_NOPE_DIM + MLA_ROPE_DIM
    wq = w_uq[l].reshape(MLA_Q_RANK, MLA_HEADS, hq)
    wq = jnp.pad(wq, ((0, 0), (0, 0), (0, HEAD_LANES - hq))).reshape(MLA_Q_RANK, MLA_HEADS * HEAD_LANES)
    wkv = w_ukv[l].reshape(MLA_KV_RANK, MLA_HEADS, MLA_NOPE_DIM + MLA_V_DIM)
    wk = jnp.pad(wkv[:, :, :MLA_NOPE_DIM], ((0, 0), (0, 0), (0, HEAD_LANES - MLA_NOPE_DIM)))
    wv = wkv[:, :, MLA_NOPE_DIM:]
    return dict(
        w_in=w_main, g_q=g_q[l][None], g_kv=g_kv[l][None],
        w_uq=wq.astype(BF16), w_uk=wk.reshape(MLA_KV_RANK, MLA_HEADS * HEAD_LANES).astype(BF16),
        w_uv=wv.reshape(MLA_KV_RANK, MLA_HEADS * MLA_V_DIM).astype(BF16),
        na_tab=_na_bias_table(rpb[l]),
        w_dw=w_dw[l], b_dw=b_dw[l][None], g_cn=g_cn[l][None], b_cn=b_cn[l][None],
        w_pw2=w_pw2[l].astype(BF16), w_oa=w_oa[l].astype(BF16), w_ob=w_ob[l].astype(BF16), w_out=w_out[l].astype(BF16),
        ln1_g=ln1_g[l][None], ln1_b=ln1_b[l][None], w_rt=w_router[l].T.astype(BF16),
        ln2_g=ln2_g[l][None], ln2_b=ln2_b[l][None])


def _moe(x1, h2, aff_t, mod_g2, mod_row, wl, w_gate, w_up, w_down, layer, nb):
    n = x1.shape[1]
    cap = EC_CAPACITY_FACTOR * n // N_EXPERTS
    slot, slot_t = _route(aff_t, cap)
    y = _experts(slot, aff_t, h2, w_gate, w_up, w_down, layer, cap, nb)
    return _combine(slot_t, y, x1, mod_g2, mod_row, wl, cap)


def kernel(x, c, ctx, c_ctx, w_ada, b_ada, w_in, g_q, w_uq, g_kv, w_ukv, rpb, w_dw, b_dw, g_cn, b_cn, w_pw2, w_oa,
           w_ob, w_out, ln1_g, ln1_b, w_router, w_gate, w_up, w_down, ln2_g, ln2_b):
    bsz, s, d = x.shape
    depth = w_ada.shape[0]
    ctx_row = bsz
    n_rows = -(-(bsz + 1) // SUBLANES) * SUBLANES
    cc = jnp.concatenate([c, c_ctx[None], jnp.zeros((n_rows - bsz - 1, d), F32)], axis=0)
    mod_all = _ada_all_layers(cc, w_ada, b_ada)
    rope = _rope_tables(s)
    for l in range(depth):
        last = l == depth - 1
        wl = _prep_layer(l, w_in, g_q, w_uq, g_kv, w_ukv, rpb, w_dw, b_dw, g_cn, b_cn, w_pw2, w_oa, w_ob, w_out,
                         ln1_g, ln1_b, w_router, ln2_g, ln2_b)
        mod = mod_all[l][:, None, :]
        mod_a, mod_e, mod_h = mod[:, :, 0:2 * d], mod[:, :, 2 * d:5 * d], mod[:, :, 5 * d:6 * d]
        qa, ka, va, qb, kb, vb, u, gl = _in_proj(x, mod_a, None, wl, rope, True)
        if last:
            ka_c, va_c, kb_c, vb_c = _in_proj(ctx, mod_a, ctx_row, wl, None, False)
        else:
            qa_c, ka_c, va_c, qb_c, kb_c, vb_c, u_c, gl_c = _in_proj(ctx, mod_a, ctx_row, wl, None, True)
        ya = _na_attention(qa, ka, va, ka_c, va_c, wl["na_tab"])
        yb = _mla_attention(qb, [(kb_c, vb_c), (kb, vb)])
        z = _conv_module(u, wl)
        x1, h2, aff_t = _merge(x, ya, yb, z, gl, mod_e, None, wl)
        x = _moe(x1, h2, aff_t, mod_h, None, wl, w_gate, w_up, w_down, l, 1)
        if not last:
            ya_c = _dense_attention(qa_c, [(ka_c, va_c)], True)
            yb_c = _mla_attention(qb_c, [(kb_c, vb_c)])
            z_c = _conv_module(u_c, wl)
            c1, h2_c, aff_c = _merge(ctx, ya_c, yb_c, z_c, gl_c, mod_e, ctx_row, wl)
            ctx = _moe(c1, h2_c, aff_c, mod_h, ctx_row, wl, w_gate, w_up, w_down, l, bsz)
    return x
```

```python
import functools

import jax
import jax.numpy as jnp
from jax import lax
from jax.experimental import pallas as pl
from jax.experimental.pallas import tpu as pltpu

F32 = jnp.float32
BF16 = jnp.bfloat16

D_MODEL = 1024
DEPTH = 4
GRID_W = 64
NA_HEADS = 8
NA_HEAD_DIM = 64
NA_WIN_R = 8
NA_WIN_C = 16
NA_WIDTH = NA_HEADS * NA_HEAD_DIM
MLA_HEADS = 8
MLA_Q_RANK = 256
MLA_KV_RANK = 128
MLA_NOPE_DIM = 64
MLA_ROPE_DIM = 32
MLA_V_DIM = 64
ROPE_BASE = 10000.0
CONV_DIM = 512
CONV_WIDTH = 31
N_EXPERTS = 16
EXPERT_DIM = 1024
EC_CAPACITY_FACTOR = 2
LN_EPS = 1e-5
RMS_EPS = 1e-6
DEEPNORM_ALPHA = (2 * DEPTH) ** 0.25

LANES = 128
SUBLANES = 8
VMEM_LIMIT_BYTES = 56 * 1024 * 1024

ROW_TILE = 256
NA_Q_ROWS = 4
NA_BAND_ROWS = NA_Q_ROWS + NA_WIN_R
ROUTE_ROWS = 1024
CONV_CHUNK = 64
CONV_HALO = 16
HEAD_LANES = 128
MLA_KEY_CHUNK = 4096
MLA_Q_TILE = 1024
NEG_BIG = -1e30
LOG2_E = 1.4426950408889634

C_QA, C_KA, C_VA = 0, 512, 1024
C_CQ, C_CKV, C_KR = 1536, 1792, 1920
C_CONV_A, C_CONV_G, C_GATES = 2048, 2560, 3072
IN_COLS_PAD = 6144
NA_TAB_PAD = 5
NA_TAB = 2 * NA_WIN_R - 1 + 2 * NA_TAB_PAD - 1


def _params(*sem):
    return pltpu.CompilerParams(dimension_semantics=sem, vmem_limit_bytes=VMEM_LIMIT_BYTES)


def _dot(a, b):
    return jnp.dot(a, b, preferred_element_type=F32)


def _dot_nt(a, b):
    return lax.dot_general(a, b, (((1,), (1,)), ((), ())), preferred_element_type=F32)


def _layer_norm(x, g, b):
    mu = jnp.mean(x, axis=-1, keepdims=True)
    var = jnp.mean(jnp.square(x - mu), axis=-1, keepdims=True)
    return (x - mu) * lax.rsqrt(var + LN_EPS) * g + b


def _rms_norm(x, g):
    return x * lax.rsqrt(jnp.mean(jnp.square(x), axis=-1, keepdims=True) + RMS_EPS) * g


def _silu(x):
    return x * jax.nn.sigmoid(x)


def _ada_kernel(c_ref, w_ref, b_ref, o_ref):
    s = _silu(c_ref[...]).astype(BF16)
    o_ref[0] = _dot(s, w_ref[0].astype(BF16)) + b_ref[0]


def _ada_all_layers(cc, w_ada, b_ada):
    n_layers, d, cols = w_ada.shape
    rows = cc.shape[0]
    tn = 1536
    return pl.pallas_call(
        _ada_kernel,
        grid=(n_layers, cols // tn),
        in_specs=[pl.BlockSpec((rows, d), lambda l, j: (0, 0)),
                  pl.BlockSpec((1, d, tn), lambda l, j: (l, 0, j)),
                  pl.BlockSpec((1, 1, tn), lambda l, j: (l, 0, j))],
        out_specs=pl.BlockSpec((1, rows, tn), lambda l, j: (l, 0, j)),
        out_shape=jax.ShapeDtypeStruct((n_layers, rows, cols), F32),
        compiler_params=_params("arbitrary", "arbitrary"),
        name="ada_ln",
    )(cc, w_ada, b_ada.reshape(n_layers, 1, cols))


def _rope(x, c, sa, sb):
    return x * c + pltpu.roll(x, HEAD_LANES - 8, axis=1) * sa + pltpu.roll(x, 8, axis=1) * sb


def _conv_taps(u_ext, w_ref):
    half = CONV_WIDTH // 2
    rows = CONV_CHUNK + 2 * CONV_HALO
    out = []
    for t0 in range(0, u_ext.shape[0] - 2 * CONV_HALO, CONV_CHUNK):
        accs = []
        for cb in range(CONV_DIM // LANES):
            ls = slice(cb * LANES, (cb + 1) * LANES)
            win = u_ext[t0:t0 + rows, ls]
            a = jnp.zeros((CONV_CHUNK, LANES), F32)
            for r in range(SUBLANES):
                ph = win if r == 0 else pltpu.roll(win, rows - r, axis=0)
                for lo in range(r, CONV_HALO - half + CONV_WIDTH, SUBLANES):
                    k = lo - (CONV_HALO - half)
                    if k >= 0:
                        a = a + ph[lo - r:lo - r + CONV_CHUNK, :] * w_ref[k:k + 1, ls]
            accs.append(a)
        out.append(jnp.concatenate(accs, axis=1))
    return jnp.concatenate(out, axis=0)


def _in_proj_kernel(*refs, use_rope, full):
    n_in = 8 + (3 if use_rope else 0) + (6 if full else 0)
    ins, outs = refs[:n_in], refs[n_in:]
    x_ref, mod_ref, w_ref, gq_ref, gkv_ref, wuq_ref, wuk_ref, wuv_ref = ins[:8]
    pos = 8
    if use_rope:
        rc_ref, rsa_ref, rsb_ref = ins[pos:pos + 3]
        pos += 3
    if full:
        xp_ref, xn_ref, wdw_ref, bdw_ref, gcn_ref, bcn_ref = ins[pos:pos + 6]
        qa_ref, ka_ref, va_ref, qb_ref, kb_ref, vb_ref, z_ref, gl_ref = outs
    else:
        ka_ref, va_ref, kb_ref, vb_ref = outs
    d = D_MODEL
    mod = mod_ref[0]
    modulate = lambda xv: (xv * (1.0 + mod[:, d:2 * d]) + mod[:, 0:d]).astype(BF16)
    h = modulate(x_ref[0])

    def proj(c0, width):
        return _dot(h, w_ref[:, c0:c0 + width])

    if full:
        i = pl.program_id(1)
        h_ext = jnp.concatenate([modulate(xp_ref[0]), h, modulate(xn_ref[0])], axis=0)
        u_ext = (_dot(h_ext, w_ref[:, C_CONV_A:C_CONV_A + CONV_DIM])
                 * jax.nn.sigmoid(_dot(h_ext, w_ref[:, C_CONV_G:C_CONV_G + CONV_DIM])))
        row = lax.broadcasted_iota(jnp.int32, u_ext.shape, 0)
        inside = ((row >= CONV_HALO) | (i > 0)) & ((row < u_ext.shape[0] - CONV_HALO) | (i < pl.num_programs(1) - 1))
        conv = _conv_taps(jnp.where(inside, u_ext, 0.0), wdw_ref)
        z_ref[0] = _silu(_layer_norm(conv + bdw_ref[...], gcn_ref[...], bcn_ref[...])).astype(BF16)
    ka_ref[0] = proj(C_KA, NA_WIDTH).astype(BF16)
    va_ref[0] = proj(C_VA, NA_WIDTH).astype(BF16)
    if use_rope:
        rc, rsa, rsb = rc_ref[...], rsa_ref[...], rsb_ref[...]
    n_kv = _rms_norm(proj(C_CKV, MLA_KV_RANK), gkv_ref[...]).astype(BF16)
    kr = proj(C_KR, HEAD_LANES)
    if use_rope:
        kr = _rope(kr, rc, rsa, rsb)
    k_nope = _dot(n_kv, wuk_ref[...])
    for hd in range(MLA_HEADS):
        sl = slice(hd * HEAD_LANES, (hd + 1) * HEAD_LANES)
        kb_ref[0, :, sl] = (k_nope[:, sl] + kr).astype(BF16)
    v_t = _dot(n_kv, wuv_ref[...]).T
    ones = jnp.ones((MLA_V_DIM, v_t.shape[1]), BF16)
    for hd in range(MLA_HEADS):
        vb_ref[0, hd * HEAD_LANES:hd * HEAD_LANES + MLA_V_DIM, :] = v_t[hd * MLA_V_DIM:(hd + 1) * MLA_V_DIM, :].astype(BF16)
        vb_ref[0, hd * HEAD_LANES + MLA_V_DIM:(hd + 1) * HEAD_LANES, :] = ones
    if not full:
        return
    qa_ref[0] = (proj(C_QA, NA_WIDTH) * (LOG2_E * NA_HEAD_DIM ** -0.5)).astype(BF16)
    n_q = _rms_norm(proj(C_CQ, MLA_Q_RANK), gq_ref[...]).astype(BF16)
    q = _dot(n_q, wuq_ref[...])
    mla_scale = LOG2_E * (MLA_NOPE_DIM + MLA_ROPE_DIM) ** -0.5
    for hd in range(MLA_HEADS):
        sl = slice(hd * HEAD_LANES, (hd + 1) * HEAD_LANES)
        qh = q[:, sl]
        if use_rope:
            qh = _rope(qh, rc, rsa, rsb)
        qb_ref[0, :, sl] = (qh * mla_scale).astype(BF16)
    gl_ref[0] = proj(C_GATES, 3 * d)


def _in_proj(x, mod, mod_row, wl, rope, full):
    bsz, n, d = x.shape
    tm = ROW_TILE
    use_rope = rope is not None
    if mod_row is None:
        mod_map = lambda b, i: (b, 0, 0)
    else:
        mod_map = lambda b, i: (mod_row, 0, 0)
    const2 = lambda b, i: (0, 0)
    row_spec = lambda w: pl.BlockSpec((1, tm, w), lambda b, i: (b, i, 0))
    in_specs = [row_spec(d),
                pl.BlockSpec((1, 1, 2 * d), mod_map),
                pl.BlockSpec((d, IN_COLS_PAD), const2, pipeline_mode=pl.Buffered(1)),
                pl.BlockSpec((1, MLA_Q_RANK), const2),
                pl.BlockSpec((1, MLA_KV_RANK), const2),
                pl.BlockSpec((MLA_Q_RANK, MLA_HEADS * HEAD_LANES), const2),
                pl.BlockSpec((MLA_KV_RANK, MLA_HEADS * HEAD_LANES), const2),
                pl.BlockSpec((MLA_KV_RANK, MLA_HEADS * MLA_V_DIM), const2)]
    args = [x, mod, wl["w_in"], wl["g_q"], wl["g_kv"], wl["w_uq"], wl["w_uk"], wl["w_uv"]]
    if use_rope:
        in_specs += [pl.BlockSpec((tm, HEAD_LANES), lambda b, i: (i, 0))] * 3
        args += list(rope)
    if full:
        per_tile = tm // CONV_HALO
        last_halo = n // CONV_HALO - 1
        vec = pl.BlockSpec((1, CONV_DIM), const2)
        in_specs += [pl.BlockSpec((1, CONV_HALO, d), lambda b, i: (b, jnp.maximum(i * per_tile - 1, 0), 0)),
                     pl.BlockSpec((1, CONV_HALO, d), lambda b, i: (b, jnp.minimum((i + 1) * per_tile, last_halo), 0)),
                     pl.BlockSpec((CONV_WIDTH, CONV_DIM), const2), vec, vec, vec]
        args += [x, x, wl["w_dw"], wl["b_dw"], wl["g_cn"], wl["b_cn"]]
    sd = lambda w, dt: jax.ShapeDtypeStruct((bsz, n, w), dt)
    kv_shapes = [sd(NA_WIDTH, BF16), sd(NA_WIDTH, BF16), sd(MLA_HEADS * HEAD_LANES, BF16),
                 jax.ShapeDtypeStruct((bsz, MLA_HEADS * HEAD_LANES, n), BF16)]
    kv_specs = [row_spec(NA_WIDTH), row_spec(NA_WIDTH), row_spec(MLA_HEADS * HEAD_LANES),
                pl.BlockSpec((1, MLA_HEADS * HEAD_LANES, tm), lambda b, i: (b, 0, i))]
    if full:
        out_shape = [sd(NA_WIDTH, BF16), kv_shapes[0], kv_shapes[1], sd(MLA_HEADS * HEAD_LANES, BF16), kv_shapes[2],
                     kv_shapes[3], sd(CONV_DIM, BF16), sd(3 * d, F32)]
        out_specs = [row_spec(NA_WIDTH), kv_specs[0], kv_specs[1], row_spec(MLA_HEADS * HEAD_LANES), kv_specs[2],
                     kv_specs[3], row_spec(CONV_DIM), row_spec(3 * d)]
    else:
        out_shape, out_specs = kv_shapes, kv_specs
    return pl.pallas_call(
        functools.partial(_in_proj_kernel, use_rope=use_rope, full=full),
        grid=(bsz, n // tm),
        in_specs=in_specs,
        out_specs=out_specs,
        out_shape=out_shape,
        compiler_params=_params("parallel", "arbitrary"),
        name="in_proj",
    )(*args)


def _pair_lane_mask(rows):
    return lax.broadcasted_iota(jnp.int32, (rows, LANES), 1) < NA_HEAD_DIM


def _ones_in_other_half(v_pair, hh):
    keep = _pair_lane_mask(v_pair.shape[0])
    if hh == 1:
        keep = jnp.logical_not(keep)
    return jnp.where(keep, v_pair, jnp.ones_like(v_pair))


def _normalise_by_other_half(o):
    return o / pltpu.roll(o, NA_HEAD_DIM, axis=1)


def _na_kernel(q_ref, k_ref, v_ref, kc_ref, vc_ref, tab_ref, o_ref, *, grid_rows):
    i = pl.program_id(1)
    tq = NA_Q_ROWS * GRID_W
    nband = NA_BAND_ROWS * GRID_W
    r0 = jnp.clip(NA_Q_ROWS * i - NA_WIN_R // 2, 0, grid_rows - NA_BAND_ROWS)
    kstart = pl.multiple_of(r0 * GRID_W, GRID_W)
    qrow = NA_Q_ROWS * i + lax.broadcasted_iota(jnp.int32, (tq, nband), 0) // GRID_W
    krow = r0 + lax.broadcasted_iota(jnp.int32, (tq, nband), 1) // GRID_W
    rstart = jnp.clip(qrow - NA_WIN_R // 2, 0, grid_rows - NA_WIN_R)
    row_mask = jnp.where((krow >= rstart) & (krow < rstart + NA_WIN_R), 0.0, NEG_BIG)
    tab_off = r0 - NA_Q_ROWS * i + (NA_WIN_R - 1) + NA_TAB_PAD
    first = _pair_lane_mask(tq)
    for p in range(NA_HEADS // 2):
        ls = slice(p * LANES, (p + 1) * LANES)
        qp = q_ref[0, :, ls]
        kband = k_ref[0, pl.ds(kstart, nband), ls]
        vband = v_ref[0, pl.ds(kstart, nband), ls]
        kctx = kc_ref[0, :, ls]
        vctx = vc_ref[0, :, ls]
        halves = []
        for hh in range(2):
            hd = 2 * p + hh
            qm = jnp.where(first if hh == 0 else jnp.logical_not(first), qp, jnp.zeros_like(qp))
            sw = _dot_nt(qm, kband)
            sc = _dot_nt(qm, kctx)
            bias_rows = []
            for a in range(NA_Q_ROWS):
                blocks = [tab_ref[hd * NA_TAB + tab_off + 2 * m - a] for m in range(NA_BAND_ROWS // 2)]
                bias_rows.append(jnp.concatenate(blocks, axis=1))
            sw = sw + (jnp.concatenate(bias_rows, axis=0) + row_mask)
            mx = jnp.maximum(jnp.max(sw, axis=1, keepdims=True), jnp.max(sc, axis=1, keepdims=True))
            pw = jnp.exp2(sw - mx)
            pc = jnp.exp2(sc - mx)
            o = (_dot(pw.astype(BF16), _ones_in_other_half(vband, hh))
                 + _dot(pc.astype(BF16), _ones_in_other_half(vctx, hh)))
            halves.append(_normalise_by_other_half(o))
        o_ref[0, :, ls] = jnp.where(first, halves[0], halves[1]).astype(BF16)


def _na_attention(qa, ka, va, ka_c, va_c, tab):
    bsz, s, w = qa.shape
    n_ctx = ka_c.shape[1]
    grid_rows = s // GRID_W
    assert grid_rows >= NA_BAND_ROWS and grid_rows % NA_Q_ROWS == 0
    tq = NA_Q_ROWS * GRID_W
    return pl.pallas_call(
        functools.partial(_na_kernel, grid_rows=grid_rows),
        grid=(bsz, s // tq),
        in_specs=[pl.BlockSpec((1, tq, w), lambda b, i: (b, i, 0)),
                  pl.BlockSpec((1, s, w), lambda b, i: (b, 0, 0)),
                  pl.BlockSpec((1, s, w), lambda b, i: (b, 0, 0)),
                  pl.BlockSpec((1, n_ctx, w), lambda b, i: (b, 0, 0)),
                  pl.BlockSpec((1, n_ctx, w), lambda b, i: (b, 0, 0)),
                  pl.BlockSpec(tab.shape, lambda b, i: (0, 0, 0), pipeline_mode=pl.Buffered(1))],
        out_specs=pl.BlockSpec((1, tq, w), lambda b, i: (b, i, 0)),
        out_shape=jax.ShapeDtypeStruct((bsz, s, w), BF16),
        compiler_params=_params("parallel", "arbitrary"),
        name="na_attn",
    )(qa, ka, va, ka_c, va_c, tab)


def _dense_attn_kernel(*refs, n_seg, paired_q):
    q_ref, kv_refs, o_ref = refs[0], refs[1:1 + 2 * n_seg], refs[1 + 2 * n_seg]
    tq = q_ref.shape[1]
    first = _pair_lane_mask(tq)
    n_heads = o_ref.shape[2] // NA_HEAD_DIM
    for p in range(n_heads // 2):
        ls = slice(p * LANES, (p + 1) * LANES)
        halves = []
        for hh in range(2):
            hd = 2 * p + hh
            if paired_q:
                qp = q_ref[0, :, ls]
                qh = jnp.where(first if hh == 0 else jnp.logical_not(first), qp, jnp.zeros_like(qp))
                ks = ls
            else:
                ks = slice(hd * HEAD_LANES, (hd + 1) * HEAD_LANES)
                qh = q_ref[0, :, ks]
            scores = [_dot_nt(qh, kv_refs[2 * j][0, :, ks]) for j in range(n_seg)]
            mx = functools.reduce(jnp.maximum, [jnp.max(sj, axis=1, keepdims=True) for sj in scores])
            o = functools.reduce(lambda a, b: a + b,
                                 [_dot(jnp.exp2(scores[j] - mx).astype(BF16), _ones_in_other_half(kv_refs[2 * j + 1][0, :, ls], hh))
                                  for j in range(n_seg)])
            halves.append(_normalise_by_other_half(o))
        o_ref[0, :, ls] = jnp.where(first, halves[0], halves[1]).astype(BF16)


def _mla_attn_kernel(*refs, n_seg):
    q_ref, kv_refs, o_ref = refs[0], refs[1:1 + 2 * n_seg], refs[1 + 2 * n_seg]
    chunks = []
    for j in range(n_seg):
        nk = kv_refs[2 * j].shape[1]
        step = nk if nk <= MLA_KEY_CHUNK else MLA_KEY_CHUNK
        chunks += [(j, c0, step) for c0 in range(0, nk, step)]

    def scores_of(hd):
        hs = slice(hd * HEAD_LANES, (hd + 1) * HEAD_LANES)
        qh = q_ref[0, :, hs]
        return [_dot_nt(kv_refs[2 * j][0, c0:c0 + sz, hs], qh) for j, c0, sz in chunks]

    def finish(hd, scores):
        hs = slice(hd * HEAD_LANES, (hd + 1) * HEAD_LANES)
        mx = functools.reduce(jnp.maximum, [jnp.max(sj, axis=0, keepdims=True) for sj in scores])
        o_t = functools.reduce(lambda a, b: a + b,
                               [_dot(kv_refs[2 * j + 1][0, hs, c0:c0 + sz], jnp.exp2(sj - mx).astype(BF16))
                                for (j, c0, sz), sj in zip(chunks, scores)])
        return o_t[0:MLA_V_DIM, :] / o_t[MLA_V_DIM:HEAD_LANES, :]

    outs = []
    nxt = scores_of(0)
    for hd in range(MLA_HEADS):
        cur = nxt
        if hd + 1 < MLA_HEADS:
            nxt = scores_of(hd + 1)
        outs.append(finish(hd, cur))
    for p in range(MLA_HEADS // 2):
        o_ref[0, :, p * LANES:(p + 1) * LANES] = jnp.concatenate(outs[2 * p:2 * p + 2], axis=0).T.astype(BF16)


def _mla_attention(q, kvs):
    bsz, nq, wq = q.shape
    tq = min(nq, MLA_Q_TILE)
    in_specs = [pl.BlockSpec((1, tq, wq), lambda b, i: (b, i, 0))]
    args = [q]
    for k, v_t in kvs:
        in_specs += [pl.BlockSpec((1, k.shape[1], k.shape[2]), lambda b, i: (b, 0, 0)),
                     pl.BlockSpec((1, v_t.shape[1], v_t.shape[2]), lambda b, i: (b, 0, 0))]
        args += [k, v_t]
    return pl.pallas_call(
        functools.partial(_mla_attn_kernel, n_seg=len(kvs)),
        grid=(bsz, nq // tq),
        in_specs=in_specs,
        out_specs=pl.BlockSpec((1, tq, MLA_HEADS * MLA_V_DIM), lambda b, i: (b, i, 0)),
        out_shape=jax.ShapeDtypeStruct((bsz, nq, MLA_HEADS * MLA_V_DIM), BF16),
        compiler_params=_params("parallel", "arbitrary"),
        name="mla_attn",
    )(*args)


def _dense_attention(q, kvs, paired_q):
    bsz, nq, wq = q.shape
    tq = ROW_TILE
    wv = kvs[0][1].shape[2]
    in_specs = [pl.BlockSpec((1, tq, wq), lambda b, i: (b, i, 0))]
    args = [q]
    for k, v in kvs:
        in_specs += [pl.BlockSpec((1, k.shape[1], k.shape[2]), lambda b, i: (b, 0, 0)),
                     pl.BlockSpec((1, v.shape[1], v.shape[2]), lambda b, i: (b, 0, 0))]
        args += [k, v]
    return pl.pallas_call(
        functools.partial(_dense_attn_kernel, n_seg=len(kvs), paired_q=paired_q),
        grid=(bsz, nq // tq),
        in_specs=in_specs,
        out_specs=pl.BlockSpec((1, tq, wv), lambda b, i: (b, i, 0)),
        out_shape=jax.ShapeDtypeStruct((bsz, nq, wv), BF16),
        compiler_params=_params("parallel", "arbitrary"),
        name="dense_attn",
    )(*args)


def _merge_kernel(x_ref, ya_ref, yb_ref, z_ref, gl_ref, mod_ref, woa_ref, wob_ref, wpw_ref, wout_ref,
                  g_ref, b_ref, wrt_ref, x1_ref, h2_ref, aff_ref):
    d = D_MODEL
    gl = gl_ref[0]
    m = (jax.nn.sigmoid(gl[:, 0:d]) * _dot(ya_ref[0], woa_ref[...])
         + jax.nn.sigmoid(gl[:, d:2 * d]) * _dot(yb_ref[0], wob_ref[...])
         + jax.nn.sigmoid(gl[:, 2 * d:3 * d]) * _dot(z_ref[0], wpw_ref[...]))
    y = _dot(m.astype(BF16), wout_ref[...])
    mod = mod_ref[0]
    x1 = _layer_norm(DEEPNORM_ALPHA * x_ref[0] + mod[:, 0:d] * y, g_ref[...], b_ref[...])
    x1_ref[0] = x1
    h2 = (x1 * (1.0 + mod[:, 2 * d:3 * d]) + mod[:, d:2 * d]).astype(BF16)
    h2_ref[0] = h2
    logits = _dot_nt(wrt_ref[...], h2)
    e = jnp.exp(logits - jnp.max(logits, axis=0, keepdims=True))
    aff_ref[0] = e / jnp.sum(e, axis=0, keepdims=True)


def _merge(x, ya, yb, z, gl, mod, mod_row, wl):
    bsz, n, d = x.shape
    tm = ROW_TILE
    if mod_row is None:
        mod_map = lambda b, i: (b, 0, 0)
    else:
        mod_map = lambda b, i: (mod_row, 0, 0)
    row = lambda w: pl.BlockSpec((1, tm, w), lambda b, i: (b, i, 0))
    const2 = lambda b, i: (0, 0)
    wspec = lambda r: pl.BlockSpec((r, d), const2)
    return pl.pallas_call(
        _merge_kernel,
        grid=(bsz, n // tm),
        in_specs=[row(d), row(NA_WIDTH), row(NA_WIDTH), row(CONV_DIM), row(3 * d),
                  pl.BlockSpec((1, 1, 3 * d), mod_map),
                  wspec(NA_WIDTH), wspec(NA_WIDTH), wspec(CONV_DIM), wspec(d),
                  pl.BlockSpec((1, d), const2), pl.BlockSpec((1, d), const2),
                  pl.BlockSpec((N_EXPERTS, d), const2)],
        out_specs=[row(d), row(d), pl.BlockSpec((1, N_EXPERTS, tm), lambda b, i: (b, 0, i))],
        out_shape=[jax.ShapeDtypeStruct((bsz, n, d), F32), jax.ShapeDtypeStruct((bsz, n, d), BF16),
                   jax.ShapeDtypeStruct((bsz, N_EXPERTS, n), F32)],
        compiler_params=_params("parallel", "arbitrary"),
        name="merge_ln1_router",
    )(x, ya, yb, z, gl, mod, wl["w_oa"], wl["w_ob"], wl["w_pw2"], wl["w_out"], wl["ln1_g"], wl["ln1_b"], wl["w_rt"])


def _lane_cumsum(v):
    n = v.shape[1]
    lane = lax.broadcasted_iota(jnp.int32, v.shape, 1)
    s = 1
    while s < n:
        v = v + jnp.where(lane >= s, pltpu.roll(v, s, axis=1), 0)
        s *= 2
    return v


def _route_kernel(aff_ref, slot_ref, slot_t_ref, *, cap):
    nb, e, n = aff_ref.shape
    bits = pltpu.bitcast(aff_ref[...].reshape(nb * e, n), jnp.int32)

    def search(it, thr):
        cand = thr | lax.shift_left(jnp.int32(1), 30 - it)
        cnt = jnp.sum(jnp.where(bits >= cand, 1.0, 0.0), axis=1, keepdims=True)
        return jnp.where(cnt >= cap, cand, thr)

    thr = lax.fori_loop(0, 31, search, jnp.zeros((nb * e, 1), jnp.int32))
    gt = jnp.where(bits > thr, 1, 0)
    eq = jnp.where(bits == thr, 1, 0)
    inc = _lane_cumsum(gt + eq * 65536)
    gt_before = (inc & 65535) - gt
    eq_before = lax.shift_right_logical(inc, 16) - eq
    need = cap - jnp.sum(gt, axis=1, keepdims=True)
    sel = (gt == 1) | ((eq == 1) & (eq_before < need))
    slot = jnp.where(sel, gt_before + jnp.minimum(eq_before, need), -1)
    slot_ref[...] = slot.reshape(nb, e, n)
    filler = jnp.full((LANES - e, n), -1.0, F32)
    for j in range(nb):
        padded = jnp.concatenate([slot[j * e:(j + 1) * e, :].astype(F32), filler], axis=0)
        slot_t_ref[j] = padded.T.astype(jnp.int32)


def _route(aff_t, cap):
    bsz, e, n = aff_t.shape
    nb = min(bsz, max(1, ROUTE_ROWS // (e * n // LANES)))
    return pl.pallas_call(
        functools.partial(_route_kernel, cap=cap),
        grid=(bsz // nb,),
        in_specs=[pl.BlockSpec((nb, e, n), lambda b: (b, 0, 0))],
        out_specs=[pl.BlockSpec((nb, e, n), lambda b: (b, 0, 0)), pl.BlockSpec((nb, n, LANES), lambda b: (b, 0, 0))],
        out_shape=[jax.ShapeDtypeStruct((bsz, e, n), jnp.int32), jax.ShapeDtypeStruct((bsz, n, LANES), jnp.int32)],
        compiler_params=_params("parallel"),
        name="route",
    )(aff_t)


def _expert_kernel(slot_ref, aff_ref, h_ref, wg_ref, wu_ref, wd_ref, y_ref, wg_s, wu_s, wd_s, xg_s, ac_s, *, cap):
    @pl.when(pl.program_id(1) == 0)
    def _():
        wg_s[...] = wg_ref[0, 0].astype(BF16)
        wu_s[...] = wu_ref[0, 0].astype(BF16)
        wd_s[...] = wd_ref[0, 0].astype(BF16)

    nb, n = h_ref.shape[0], h_ref.shape[1]
    c_iota = lax.broadcasted_iota(jnp.int32, (cap, n), 0)
    for j in range(nb):
        hit = slot_ref[j, 0] == c_iota
        onehot = jnp.where(hit, 1.0, 0.0).astype(BF16)
        xg_s[j * cap:(j + 1) * cap, :] = _dot(onehot, h_ref[j]).astype(BF16)
        ac_s[j * cap:(j + 1) * cap, :] = jnp.sum(jnp.where(hit, aff_ref[j, 0], 0.0), axis=1, keepdims=True)
    xg = xg_s[...]
    hid = _silu(_dot(xg, wg_s[...])) * _dot(xg, wu_s[...])
    y = _dot(hid.astype(BF16), wd_s[...]) * ac_s[...]
    for j in range(nb):
        y_ref[j, 0] = y[j * cap:(j + 1) * cap, :].astype(BF16)


def _experts(slot, aff_t, h2, w_gate, w_up, w_down, layer, cap, nb):
    bsz, e, n = slot.shape
    d, f = w_gate.shape[2], w_gate.shape[3]
    slot4 = slot.reshape(bsz, e, 1, n)
    aff4 = aff_t.reshape(bsz, e, 1, n)
    return pl.pallas_call(
        functools.partial(_expert_kernel, cap=cap),
        grid=(e, bsz // nb),
        in_specs=[pl.BlockSpec((nb, 1, 1, n), lambda ei, bi: (bi, ei, 0, 0)),
                  pl.BlockSpec((nb, 1, 1, n), lambda ei, bi: (bi, ei, 0, 0)),
                  pl.BlockSpec((nb, n, d), lambda ei, bi: (bi, 0, 0)),
                  pl.BlockSpec((1, 1, d, f), lambda ei, bi: (layer, ei, 0, 0)),
                  pl.BlockSpec((1, 1, d, f), lambda ei, bi: (layer, ei, 0, 0)),
                  pl.BlockSpec((1, 1, f, d), lambda ei, bi: (layer, ei, 0, 0))],
        out_specs=pl.BlockSpec((nb, 1, cap, d), lambda ei, bi: (bi, ei, 0, 0)),
        out_shape=jax.ShapeDtypeStruct((bsz, e, cap, d), BF16),
        scratch_shapes=[pltpu.VMEM((d, f), BF16), pltpu.VMEM((d, f), BF16), pltpu.VMEM((f, d), BF16),
                        pltpu.VMEM((nb * cap, d), BF16), pltpu.VMEM((nb * cap, 1), F32)],
        compiler_params=_params("arbitrary", "arbitrary"),
        name="experts",
    )(slot4, aff4, h2, w_gate, w_up, w_down)


def _combine_kernel(st_ref, y_ref, x_ref, mod_ref, g_ref, b_ref, o_ref, *, cap):
    st = st_ref[0]
    rows = st.shape[0]
    if cap % LANES == 0:
        c_iota = lax.broadcasted_iota(jnp.int32, (rows, cap), 1)
        scat = jnp.concatenate(
            [jnp.where(st[:, e:e + 1] == c_iota, 1.0, 0.0).astype(BF16) for e in range(N_EXPERTS)], axis=1)
    else:
        j_iota = lax.broadcasted_iota(jnp.int32, (rows, N_EXPERTS * cap), 1)
        hit = None
        for e in range(N_EXPERTS):
            col = st[:, e:e + 1]
            he = jnp.where(col >= 0, col + e * cap, -1) == j_iota
            hit = he if hit is None else (hit | he)
        scat = jnp.where(hit, 1.0, 0.0).astype(BF16)
    moe = _dot(scat, y_ref[0])
    o_ref[0] = _layer_norm(DEEPNORM_ALPHA * x_ref[0] + mod_ref[0] * moe, g_ref[...], b_ref[...])


def _combine(slot_t, y, x1, mod, mod_row, wl, cap):
    bsz, n, d = x1.shape
    tm = ROW_TILE
    if mod_row is None:
        mod_map = lambda b, i: (b, 0, 0)
    else:
        mod_map = lambda b, i: (mod_row, 0, 0)
    ec = N_EXPERTS * cap
    return pl.pallas_call(
        functools.partial(_combine_kernel, cap=cap),
        grid=(bsz, n // tm),
        in_specs=[pl.BlockSpec((1, tm, LANES), lambda b, i: (b, i, 0)),
                  pl.BlockSpec((1, ec, d), lambda b, i: (b, 0, 0)),
                  pl.BlockSpec((1, tm, d), lambda b, i: (b, i, 0)),
                  pl.BlockSpec((1, 1, d), mod_map),
                  pl.BlockSpec((1, d), lambda b, i: (0, 0)), pl.BlockSpec((1, d), lambda b, i: (0, 0))],
        out_specs=pl.BlockSpec((1, tm, d), lambda b, i: (b, i, 0)),
        out_shape=jax.ShapeDtypeStruct((bsz, n, d), F32),
        compiler_params=_params("parallel", "arbitrary"),
        name="combine_ln2",
    )(slot_t, y.reshape(bsz, ec, d), x1, mod, wl["ln2_g"], wl["ln2_b"])


def _rope_tables(n_tokens):
    t = jnp.arange(n_tokens)
    row = (t // GRID_W).astype(F32)
    col = (t % GRID_W).astype(F32)
    n_freq = MLA_ROPE_DIM // 4
    inv_freq = ROPE_BASE ** (-jnp.arange(n_freq, dtype=F32) / n_freq)
    ar, ac = row[:, None] * inv_freq, col[:, None] * inv_freq
    z8 = jnp.zeros((n_tokens, n_freq), F32)
    ones = lambda w: jnp.ones((n_tokens, w), F32)
    zeros = lambda w: jnp.zeros((n_tokens, w), F32)
    cos_r, sin_r, cos_c, sin_c = jnp.cos(ar), jnp.sin(ar), jnp.cos(ac), jnp.sin(ac)
    tail = HEAD_LANES - MLA_NOPE_DIM - MLA_ROPE_DIM
    rc = jnp.concatenate([ones(MLA_NOPE_DIM), cos_r, cos_r, cos_c, cos_c, ones(tail)], axis=1)
    rsa = jnp.concatenate([zeros(MLA_NOPE_DIM), -sin_r, z8, -sin_c, z8, zeros(tail)], axis=1)
    rsb = jnp.concatenate([zeros(MLA_NOPE_DIM), z8, sin_r, z8, sin_c, zeros(tail)], axis=1)
    return rc, rsa, rsb


def _na_bias_table(rpb_l):
    col = jnp.arange(GRID_W)
    c_start = jnp.clip(col - NA_WIN_C // 2, 0, GRID_W - NA_WIN_C)
    col_mask = (col[None, :] >= c_start[:, None]) & (col[None, :] < c_start[:, None] + NA_WIN_C)
    dc_idx = jnp.clip(col[None, :] - col[:, None], -(NA_WIN_C - 1), NA_WIN_C - 1) + NA_WIN_C - 1
    blocks = jnp.where(col_mask[None, None], rpb_l[:, :, dc_idx] * LOG2_E, NEG_BIG)
    blocks = jnp.pad(blocks, ((0, 0), (NA_TAB_PAD, NA_TAB_PAD), (0, 0), (0, 0)), constant_values=NEG_BIG)
    pairs = jnp.concatenate([blocks[:, :-1], blocks[:, 1:]], axis=-1)
    return pairs.reshape(NA_HEADS * NA_TAB, GRID_W, 2 * GRID_W).astype(F32)


def _prep_layer(l, w_in, g_q, w_uq, g_kv, w_ukv, rpb, w_dw, b_dw, g_cn, b_cn, w_pw2, w_oa, w_ob, w_out,
                ln1_g, ln1_b, w_router, ln2_g, ln2_b):
    d = D_MODEL
    wi = w_in[l]
    off = [0, 512, 1024, 1536, 1792, 1920, 1952, 2976, 6048]
    zpad = lambda w: jnp.zeros((d, w), F32)
    w_main = jnp.concatenate([
        wi[:, off[0]:off[5]],
        zpad(MLA_NOPE_DIM), wi[:, off[5]:off[6]], zpad(HEAD_LANES - MLA_NOPE_DIM - MLA_ROPE_DIM),
        wi[:, off[6]:off[8]],
    ], axis=1).astype(BF16)
    hq = MLA_NOPE_DIM + MLA_ROPE_DIM
    wq = w_uq[l].reshape(MLA_Q_RANK, MLA_HEADS, hq)
    wq = jnp.pad(wq, ((0, 0), (0, 0), (0, HEAD_LANES - hq))).reshape(MLA_Q_RANK, MLA_HEADS * HEAD_LANES)
    wkv = w_ukv[l].reshape(MLA_KV_RANK, MLA_HEADS, MLA_NOPE_DIM + MLA_V_DIM)
    wk = jnp.pad(wkv[:, :, :MLA_NOPE_DIM], ((0, 0), (0, 0), (0, HEAD_LANES - MLA_NOPE_DIM)))
    wv = wkv[:, :, MLA_NOPE_DIM:]
    return dict(
        w_in=w_main, g_q=g_q[l][None], g_kv=g_kv[l][None],
        w_uq=wq.astype(BF16), w_uk=wk.reshape(MLA_KV_RANK, MLA_HEADS * HEAD_LANES).astype(BF16),
        w_uv=wv.reshape(MLA_KV_RANK, MLA_HEADS * MLA_V_DIM).astype(BF16),
        na_tab=_na_bias_table(rpb[l]),
        w_dw=w_dw[l], b_dw=b_dw[l][None], g_cn=g_cn[l][None], b_cn=b_cn[l][None],
        w_pw2=w_pw2[l].astype(BF16), w_oa=w_oa[l].astype(BF16), w_ob=w_ob[l].astype(BF16), w_out=w_out[l].astype(BF16),
        ln1_g=ln1_g[l][None], ln1_b=ln1_b[l][None], w_rt=w_router[l].T.astype(BF16),
        ln2_g=ln2_g[l][None], ln2_b=ln2_b[l][None])


def _moe(x1, h2, aff_t, mod_g2, mod_row, wl, w_gate, w_up, w_down, layer, nb):
    n = x1.shape[1]
    cap = EC_CAPACITY_FACTOR * n // N_EXPERTS
    slot, slot_t = _route(aff_t, cap)
    y = _experts(slot, aff_t, h2, w_gate, w_up, w_down, layer, cap, nb)
    return _combine(slot_t, y, x1, mod_g2, mod_row, wl, cap)


def kernel(x, c, ctx, c_ctx, w_ada, b_ada, w_in, g_q, w_uq, g_kv, w_ukv, rpb, w_dw, b_dw, g_cn, b_cn, w_pw2, w_oa,
           w_ob, w_out, ln1_g, ln1_b, w_router, w_gate, w_up, w_down, ln2_g, ln2_b):
    bsz, s, d = x.shape
    depth = w_ada.shape[0]
    ctx_row = bsz
    n_rows = -(-(bsz + 1) // SUBLANES) * SUBLANES
    cc = jnp.concatenate([c, c_ctx[None], jnp.zeros((n_rows - bsz - 1, d), F32)], axis=0)
    mod_all = _ada_all_layers(cc, w_ada, b_ada)
    rope = _rope_tables(s)
    for l in range(depth):
        last = l == depth - 1
        wl = _prep_layer(l, w_in, g_q, w_uq, g_kv, w_ukv, rpb, w_dw, b_dw, g_cn, b_cn, w_pw2, w_oa, w_ob, w_out,
                         ln1_g, ln1_b, w_router, ln2_g, ln2_b)
        mod = mod_all[l][:, None, :]
        mod_a, mod_e, mod_h = mod[:, :, 0:2 * d], mod[:, :, 2 * d:5 * d], mod[:, :, 5 * d:6 * d]
        qa, ka, va, qb, kb, vb, z, gl = _in_proj(x, mod_a, None, wl, rope, True)
        if last:
            ka_c, va_c, kb_c, vb_c = _in_proj(ctx, mod_a, ctx_row, wl, None, False)
        else:
            qa_c, ka_c, va_c, qb_c, kb_c, vb_c, z_c, gl_c = _in_proj(ctx, mod_a, ctx_row, wl, None, True)
        ya = _na_attention(qa, ka, va, ka_c, va_c, wl["na_tab"])
        yb = _mla_attention(qb, [(kb_c, vb_c), (kb, vb)])
        x1, h2, aff_t = _merge(x, ya, yb, z, gl, mod_e, None, wl)
        x = _moe(x1, h2, aff_t, mod_h, None, wl, w_gate, w_up, w_down, l, 1)
        if not last:
            ya_c = _dense_attention(qa_c, [(ka_c, va_c)], True)
            yb_c = _mla_attention(qb_c, [(kb_c, vb_c)])
            c1, h2_c, aff_c = _merge(ctx, ya_c, yb_c, z_c, gl_c, mod_e, ctx_row, wl)
            ctx = _moe(c1, h2_c, aff_c, mod_h, ctx_row, wl, w_gate, w_up, w_down, l, bsz)
    return x
```

```python
import functools

import jax
import jax.numpy as jnp
from jax import lax
from jax.experimental import pallas as pl
from jax.experimental.pallas import tpu as pltpu

F32 = jnp.float32
BF16 = jnp.bfloat16

D_MODEL = 1024
DEPTH = 4
GRID_W = 64
NA_HEADS = 8
NA_HEAD_DIM = 64
NA_WIN_R = 8
NA_WIN_C = 16
NA_WIDTH = NA_HEADS * NA_HEAD_DIM
MLA_HEADS = 8
MLA_Q_RANK = 256
MLA_KV_RANK = 128
MLA_NOPE_DIM = 64
MLA_ROPE_DIM = 32
MLA_V_DIM = 64
ROPE_BASE = 10000.0
CONV_DIM = 512
CONV_WIDTH = 31
N_EXPERTS = 16
EXPERT_DIM = 1024
EC_CAPACITY_FACTOR = 2
LN_EPS = 1e-5
RMS_EPS = 1e-6
DEEPNORM_ALPHA = (2 * DEPTH) ** 0.25

LANES = 128
SUBLANES = 8
VMEM_LIMIT_BYTES = 56 * 1024 * 1024

ROW_TILE = 256
NA_Q_ROWS = 4
NA_BAND_ROWS = NA_Q_ROWS + NA_WIN_R
ROUTE_ROWS = 1024
CONV_CHUNK = 64
CONV_HALO = 16
HEAD_LANES = 128
MLA_KEY_CHUNK = 4096
MLA_Q_TILE = 1024
NEG_BIG = -1e30
LOG2_E = 1.4426950408889634

C_QA, C_KA, C_VA = 0, 512, 1024
C_CQ, C_CKV, C_KR = 1536, 1792, 1920
C_CONV_A, C_CONV_G, C_GATES = 2048, 2560, 3072
IN_COLS_PAD = 6144
NA_TAB_LO = NA_BAND_ROWS - NA_WIN_R
NA_TAB_HI = NA_BAND_ROWS - NA_WIN_R
NA_TAB = 2 * NA_WIN_R - 1 + NA_TAB_LO + NA_TAB_HI


def _params(*sem):
    return pltpu.CompilerParams(dimension_semantics=sem, vmem_limit_bytes=VMEM_LIMIT_BYTES)


def _dot(a, b):
    return jnp.dot(a, b, preferred_element_type=F32)


def _dot_nt(a, b):
    return lax.dot_general(a, b, (((1,), (1,)), ((), ())), preferred_element_type=F32)


def _layer_norm(x, g, b):
    mu = jnp.mean(x, axis=-1, keepdims=True)
    var = jnp.mean(jnp.square(x - mu), axis=-1, keepdims=True)
    return (x - mu) * lax.rsqrt(var + LN_EPS) * g + b


def _rms_norm(x, g):
    return x * lax.rsqrt(jnp.mean(jnp.square(x), axis=-1, keepdims=True) + RMS_EPS) * g


def _silu(x):
    return x * jax.nn.sigmoid(x)


def _ada_kernel(c_ref, w_ref, b_ref, o_ref):
    s = _silu(c_ref[...]).astype(BF16)
    o_ref[0] = _dot(s, w_ref[0].astype(BF16)) + b_ref[0]


def _ada_all_layers(cc, w_ada, b_ada):
    n_layers, d, cols = w_ada.shape
    rows = cc.shape[0]
    tn = 1536
    return pl.pallas_call(
        _ada_kernel,
        grid=(n_layers, cols // tn),
        in_specs=[pl.BlockSpec((rows, d), lambda l, j: (0, 0)),
                  pl.BlockSpec((1, d, tn), lambda l, j: (l, 0, j)),
                  pl.BlockSpec((1, 1, tn), lambda l, j: (l, 0, j))],
        out_specs=pl.BlockSpec((1, rows, tn), lambda l, j: (l, 0, j)),
        out_shape=jax.ShapeDtypeStruct((n_layers, rows, cols), F32),
        compiler_params=_params("arbitrary", "arbitrary"),
        name="ada_ln",
    )(cc, w_ada, b_ada.reshape(n_layers, 1, cols))


def _rope(x, c, sa, sb):
    return x * c + pltpu.roll(x, HEAD_LANES - 8, axis=1) * sa + pltpu.roll(x, 8, axis=1) * sb


def _conv_taps(u_ext, w_ref):
    half = CONV_WIDTH // 2
    rows = CONV_CHUNK + 2 * CONV_HALO
    out = []
    for t0 in range(0, u_ext.shape[0] - 2 * CONV_HALO, CONV_CHUNK):
        accs = []
        for cb in range(CONV_DIM // LANES):
            ls = slice(cb * LANES, (cb + 1) * LANES)
            win = u_ext[t0:t0 + rows, ls]
            a = jnp.zeros((CONV_CHUNK, LANES), F32)
            for r in range(SUBLANES):
                ph = win if r == 0 else pltpu.roll(win, rows - r, axis=0)
                for lo in range(r, CONV_HALO - half + CONV_WIDTH, SUBLANES):
                    k = lo - (CONV_HALO - half)
                    if k >= 0:
                        a = a + ph[lo - r:lo - r + CONV_CHUNK, :] * w_ref[k:k + 1, ls]
            accs.append(a)
        out.append(jnp.concatenate(accs, axis=1))
    return jnp.concatenate(out, axis=0)


def _store_values_t(vt_ref, v):
    v_t = v.T
    ones = jnp.ones((NA_HEAD_DIM, v_t.shape[1]), BF16)
    for hd in range(v_t.shape[0] // NA_HEAD_DIM):
        vt_ref[0, hd * HEAD_LANES:hd * HEAD_LANES + NA_HEAD_DIM, :] = v_t[hd * NA_HEAD_DIM:(hd + 1) * NA_HEAD_DIM, :].astype(BF16)
        vt_ref[0, hd * HEAD_LANES + NA_HEAD_DIM:(hd + 1) * HEAD_LANES, :] = ones


def _in_proj_kernel(*refs, use_rope, full):
    n_in = 8 + (3 if use_rope else 0) + (6 if full else 0)
    ins, outs = refs[:n_in], refs[n_in:]
    x_ref, mod_ref, w_ref, gq_ref, gkv_ref, wuq_ref, wuk_ref, wuv_ref = ins[:8]
    pos = 8
    if use_rope:
        rc_ref, rsa_ref, rsb_ref = ins[pos:pos + 3]
        pos += 3
    if full:
        xp_ref, xn_ref, wdw_ref, bdw_ref, gcn_ref, bcn_ref = ins[pos:pos + 6]
        qa_ref, ka_ref, va_ref, qb_ref, kb_ref, vb_ref, z_ref, gl_ref = outs
    else:
        ka_ref, va_ref, kb_ref, vb_ref = outs
    d = D_MODEL
    mod = mod_ref[0]
    modulate = lambda xv: (xv * (1.0 + mod[:, d:2 * d]) + mod[:, 0:d]).astype(BF16)
    h = modulate(x_ref[0])

    def proj(c0, width):
        return _dot(h, w_ref[:, c0:c0 + width])

    if full:
        i = pl.program_id(1)
        h_ext = jnp.concatenate([modulate(xp_ref[0]), h, modulate(xn_ref[0])], axis=0)
        u_ext = (_dot(h_ext, w_ref[:, C_CONV_A:C_CONV_A + CONV_DIM])
                 * jax.nn.sigmoid(_dot(h_ext, w_ref[:, C_CONV_G:C_CONV_G + CONV_DIM])))
        row = lax.broadcasted_iota(jnp.int32, u_ext.shape, 0)
        inside = ((row >= CONV_HALO) | (i > 0)) & ((row < u_ext.shape[0] - CONV_HALO) | (i < pl.num_programs(1) - 1))
        u_ext = jnp.where(inside, u_ext, 0.0)
        for t0 in range(0, h.shape[0], CONV_CHUNK):
            conv = _conv_taps(u_ext[t0:t0 + CONV_CHUNK + 2 * CONV_HALO, :], wdw_ref)
            z_ref[0, t0:t0 + CONV_CHUNK, :] = _silu(_layer_norm(conv + bdw_ref[...], gcn_ref[...], bcn_ref[...])).astype(BF16)
    ka_ref[0] = proj(C_KA, NA_WIDTH).astype(BF16)
    _store_values_t(va_ref, proj(C_VA, NA_WIDTH))
    if use_rope:
        rc, rsa, rsb = rc_ref[...], rsa_ref[...], rsb_ref[...]
    n_kv = _rms_norm(proj(C_CKV, MLA_KV_RANK), gkv_ref[...]).astype(BF16)
    kr = proj(C_KR, HEAD_LANES)
    if use_rope:
        kr = _rope(kr, rc, rsa, rsb)
    k_nope = _dot(n_kv, wuk_ref[...])
    for hd in range(MLA_HEADS):
        sl = slice(hd * HEAD_LANES, (hd + 1) * HEAD_LANES)
        kb_ref[0, :, sl] = (k_nope[:, sl] + kr).astype(BF16)
    _store_values_t(vb_ref, _dot(n_kv, wuv_ref[...]))
    if not full:
        return
    qa_ref[0] = (proj(C_QA, NA_WIDTH) * (LOG2_E * NA_HEAD_DIM ** -0.5)).astype(BF16)
    n_q = _rms_norm(proj(C_CQ, MLA_Q_RANK), gq_ref[...]).astype(BF16)
    q = _dot(n_q, wuq_ref[...])
    mla_scale = LOG2_E * (MLA_NOPE_DIM + MLA_ROPE_DIM) ** -0.5
    for hd in range(MLA_HEADS):
        sl = slice(hd * HEAD_LANES, (hd + 1) * HEAD_LANES)
        qh = q[:, sl]
        if use_rope:
            qh = _rope(qh, rc, rsa, rsb)
        qb_ref[0, :, sl] = (qh * mla_scale).astype(BF16)
    gl_ref[0] = proj(C_GATES, 3 * d)


def _in_proj(x, mod, mod_row, wl, rope, full):
    bsz, n, d = x.shape
    tm = ROW_TILE
    use_rope = rope is not None
    if mod_row is None:
        mod_map = lambda b, i: (b, 0, 0)
    else:
        mod_map = lambda b, i: (mod_row, 0, 0)
    const2 = lambda b, i: (0, 0)
    row_spec = lambda w: pl.BlockSpec((1, tm, w), lambda b, i: (b, i, 0))
    in_specs = [row_spec(d),
                pl.BlockSpec((1, 1, 2 * d), mod_map),
                pl.BlockSpec((d, IN_COLS_PAD), const2, pipeline_mode=pl.Buffered(1)),
                pl.BlockSpec((1, MLA_Q_RANK), const2),
                pl.BlockSpec((1, MLA_KV_RANK), const2),
                pl.BlockSpec((MLA_Q_RANK, MLA_HEADS * HEAD_LANES), const2),
                pl.BlockSpec((MLA_KV_RANK, MLA_HEADS * HEAD_LANES), const2),
                pl.BlockSpec((MLA_KV_RANK, MLA_HEADS * MLA_V_DIM), const2)]
    args = [x, mod, wl["w_in"], wl["g_q"], wl["g_kv"], wl["w_uq"], wl["w_uk"], wl["w_uv"]]
    if use_rope:
        in_specs += [pl.BlockSpec((tm, HEAD_LANES), lambda b, i: (i, 0))] * 3
        args += list(rope)
    if full:
        per_tile = tm // CONV_HALO
        last_halo = n // CONV_HALO - 1
        vec = pl.BlockSpec((1, CONV_DIM), const2)
        in_specs += [pl.BlockSpec((1, CONV_HALO, d), lambda b, i: (b, jnp.maximum(i * per_tile - 1, 0), 0)),
                     pl.BlockSpec((1, CONV_HALO, d), lambda b, i: (b, jnp.minimum((i + 1) * per_tile, last_halo), 0)),
                     pl.BlockSpec((CONV_WIDTH, CONV_DIM), const2), vec, vec, vec]
        args += [x, x, wl["w_dw"], wl["b_dw"], wl["g_cn"], wl["b_cn"]]
    sd = lambda w, dt: jax.ShapeDtypeStruct((bsz, n, w), dt)
    vt_shape = jax.ShapeDtypeStruct((bsz, MLA_HEADS * HEAD_LANES, n), BF16)
    vt_spec = pl.BlockSpec((1, MLA_HEADS * HEAD_LANES, tm), lambda b, i: (b, 0, i))
    kv_shapes = [sd(NA_WIDTH, BF16), vt_shape, sd(MLA_HEADS * HEAD_LANES, BF16), vt_shape]
    kv_specs = [row_spec(NA_WIDTH), vt_spec, row_spec(MLA_HEADS * HEAD_LANES), vt_spec]
    if full:
        out_shape = [sd(NA_WIDTH, BF16), kv_shapes[0], kv_shapes[1], sd(MLA_HEADS * HEAD_LANES, BF16), kv_shapes[2],
                     kv_shapes[3], sd(CONV_DIM, BF16), sd(3 * d, F32)]
        out_specs = [row_spec(NA_WIDTH), kv_specs[0], kv_specs[1], row_spec(MLA_HEADS * HEAD_LANES), kv_specs[2],
                     kv_specs[3], row_spec(CONV_DIM), row_spec(3 * d)]
    else:
        out_shape, out_specs = kv_shapes, kv_specs
    return pl.pallas_call(
        functools.partial(_in_proj_kernel, use_rope=use_rope, full=full),
        grid=(bsz, n // tm),
        in_specs=in_specs,
        out_specs=out_specs,
        out_shape=out_shape,
        compiler_params=_params("parallel", "arbitrary"),
        name="in_proj",
    )(*args)


def _pair_lane_mask(rows):
    return lax.broadcasted_iota(jnp.int32, (rows, LANES), 1) < NA_HEAD_DIM


def _na_kernel(q_ref, k_ref, vt0_ref, vt1_ref, vt2_ref, kc_ref, vct_ref, tab_ref, o_ref, *, grid_rows):
    i = pl.program_id(1)
    tq = NA_Q_ROWS * GRID_W
    nband = NA_BAND_ROWS * GRID_W
    r0 = jnp.clip(NA_Q_ROWS * i - NA_WIN_R // 2, 0, grid_rows - NA_BAND_ROWS)
    kstart = pl.multiple_of(r0 * GRID_W, GRID_W)
    krow = r0 + lax.broadcasted_iota(jnp.int32, (nband, tq), 0) // GRID_W
    qrow = NA_Q_ROWS * i + lax.broadcasted_iota(jnp.int32, (nband, tq), 1) // GRID_W
    rstart = jnp.clip(qrow - NA_WIN_R // 2, 0, grid_rows - NA_WIN_R)
    row_mask = jnp.where((krow >= rstart) & (krow < rstart + NA_WIN_R), 0.0, NEG_BIG)
    tab_off = r0 - NA_Q_ROWS * i + (NA_WIN_R - 1) + NA_TAB_LO
    first = _pair_lane_mask(tq)
    vt_refs = (vt0_ref, vt1_ref, vt2_ref)
    vt_cols = vt0_ref.shape[2]

    def scores_of(hd):
        p, hh = divmod(hd, 2)
        ls = slice(p * LANES, (p + 1) * LANES)
        qp = q_ref[0, :, ls]
        qm = jnp.where(first if hh == 0 else jnp.logical_not(first), qp, jnp.zeros_like(qp))
        return _dot_nt(k_ref[0, pl.ds(kstart, nband), ls], qm), _dot_nt(kc_ref[0, :, ls], qm)

    def finish(hd, sw, sc):
        hs = slice(hd * HEAD_LANES, (hd + 1) * HEAD_LANES)
        bias = jnp.concatenate(
            [jnp.concatenate([tab_ref[hd * NA_TAB + tab_off + j - 2 * a2] for a2 in range(NA_Q_ROWS // 2)], axis=1)
             for j in range(NA_BAND_ROWS)], axis=0)
        sw = sw + (bias + row_mask)
        mx = jnp.maximum(jnp.max(sw, axis=0, keepdims=True), jnp.max(sc, axis=0, keepdims=True))
        pw = jnp.exp2(sw - mx).astype(BF16)
        o_t = _dot(vct_ref[0, hs, :], jnp.exp2(sc - mx).astype(BF16))
        for m, vt_ref in enumerate(vt_refs):
            o_t = o_t + _dot(vt_ref[0, hs, :], pw[m * vt_cols:(m + 1) * vt_cols, :])
        return o_t[0:NA_HEAD_DIM, :] / o_t[NA_HEAD_DIM:HEAD_LANES, :]

    outs = []
    nxt = scores_of(0)
    for hd in range(NA_HEADS):
        cur = nxt
        if hd + 1 < NA_HEADS:
            nxt = scores_of(hd + 1)
        outs.append(finish(hd, *cur))
    for p in range(NA_HEADS // 2):
        o_ref[0, :, p * LANES:(p + 1) * LANES] = jnp.concatenate(outs[2 * p:2 * p + 2], axis=0).T.astype(BF16)


def _na_attention(qa, ka, va_t, ka_c, va_t_c, tab):
    bsz, s, w = qa.shape
    n_ctx = ka_c.shape[1]
    grid_rows = s // GRID_W
    assert grid_rows >= NA_BAND_ROWS and grid_rows % NA_Q_ROWS == 0 and NA_BAND_ROWS % NA_Q_ROWS == 0
    tq = NA_Q_ROWS * GRID_W
    n_vt = NA_BAND_ROWS // NA_Q_ROWS
    last_blk = grid_rows // NA_Q_ROWS - n_vt

    def vt_spec(m):
        return pl.BlockSpec((1, va_t.shape[1], tq), lambda b, i: (b, 0, jnp.clip(i - 1, 0, last_blk) + m))

    assert NA_WIN_R // 2 == NA_Q_ROWS and n_vt == 3
    return pl.pallas_call(
        functools.partial(_na_kernel, grid_rows=grid_rows),
        grid=(bsz, s // tq),
        in_specs=[pl.BlockSpec((1, tq, w), lambda b, i: (b, i, 0)),
                  pl.BlockSpec((1, s, w), lambda b, i: (b, 0, 0)),
                  vt_spec(0), vt_spec(1), vt_spec(2),
                  pl.BlockSpec((1, n_ctx, w), lambda b, i: (b, 0, 0)),
                  pl.BlockSpec((1, va_t_c.shape[1], n_ctx), lambda b, i: (b, 0, 0)),
                  pl.BlockSpec(tab.shape, lambda b, i: (0, 0, 0), pipeline_mode=pl.Buffered(1))],
        out_specs=pl.BlockSpec((1, tq, w), lambda b, i: (b, i, 0)),
        out_shape=jax.ShapeDtypeStruct((bsz, s, w), BF16),
        compiler_params=_params("parallel", "arbitrary"),
        name="na_attn",
    )(qa, ka, va_t, va_t, va_t, ka_c, va_t_c, tab)


def _mla_attn_kernel(*refs, n_seg, paired):
    q_ref, kv_refs, o_ref = refs[0], refs[1:1 + 2 * n_seg], refs[1 + 2 * n_seg]
    first = _pair_lane_mask(q_ref.shape[1])
    chunks = []
    for j in range(n_seg):
        nk = kv_refs[2 * j].shape[1]
        step = nk if nk <= MLA_KEY_CHUNK else MLA_KEY_CHUNK
        chunks += [(j, c0, step) for c0 in range(0, nk, step)]

    def scores_of(hd):
        hs = slice(hd * HEAD_LANES, (hd + 1) * HEAD_LANES)
        if paired:
            hs = slice((hd // 2) * LANES, (hd // 2 + 1) * LANES)
            qp = q_ref[0, :, hs]
            qh = jnp.where(first if hd % 2 == 0 else jnp.logical_not(first), qp, jnp.zeros_like(qp))
        else:
            qh = q_ref[0, :, hs]
        return [_dot_nt(kv_refs[2 * j][0, c0:c0 + sz, hs], qh) for j, c0, sz in chunks]

    def finish(hd, scores):
        hs = slice(hd * HEAD_LANES, (hd + 1) * HEAD_LANES)
        mx = functools.reduce(jnp.maximum, [jnp.max(sj, axis=0, keepdims=True) for sj in scores])
        o_t = functools.reduce(lambda a, b: a + b,
                               [_dot(kv_refs[2 * j + 1][0, hs, c0:c0 + sz], jnp.exp2(sj - mx).astype(BF16))
                                for (j, c0, sz), sj in zip(chunks, scores)])
        return o_t[0:MLA_V_DIM, :] / o_t[MLA_V_DIM:HEAD_LANES, :]

    outs = []
    nxt = scores_of(0)
    for hd in range(MLA_HEADS):
        cur = nxt
        if hd + 1 < MLA_HEADS:
            nxt = scores_of(hd + 1)
        outs.append(finish(hd, cur))
    for p in range(MLA_HEADS // 2):
        o_ref[0, :, p * LANES:(p + 1) * LANES] = jnp.concatenate(outs[2 * p:2 * p + 2], axis=0).T.astype(BF16)


def _mla_attention(q, kvs, paired=False):
    bsz, nq, wq = q.shape
    tq = min(nq, MLA_Q_TILE)
    in_specs = [pl.BlockSpec((1, tq, wq), lambda b, i: (b, i, 0))]
    args = [q]
    for k, v_t in kvs:
        in_specs += [pl.BlockSpec((1, k.shape[1], k.shape[2]), lambda b, i: (b, 0, 0)),
                     pl.BlockSpec((1, v_t.shape[1], v_t.shape[2]), lambda b, i: (b, 0, 0))]
        args += [k, v_t]
    return pl.pallas_call(
        functools.partial(_mla_attn_kernel, n_seg=len(kvs), paired=paired),
        grid=(bsz, nq // tq),
        in_specs=in_specs,
        out_specs=pl.BlockSpec((1, tq, MLA_HEADS * MLA_V_DIM), lambda b, i: (b, i, 0)),
        out_shape=jax.ShapeDtypeStruct((bsz, nq, MLA_HEADS * MLA_V_DIM), BF16),
        compiler_params=_params("parallel", "arbitrary"),
        name="mla_attn",
    )(*args)


def _merge_kernel(x_ref, ya_ref, yb_ref, z_ref, gl_ref, mod_ref, woa_ref, wob_ref, wpw_ref, wout_ref,
                  g_ref, b_ref, wrt_ref, x1_ref, h2_ref, aff_ref, y_s):
    d = D_MODEL
    s = pl.program_id(0)

    @pl.when(s == 0)
    def _():
        y_s[...] = jnp.zeros(y_s.shape, F32)

    prev = y_s[...]
    gl = gl_ref[0]
    m = (jax.nn.sigmoid(gl[:, 0:d]) * _dot(ya_ref[0], woa_ref[...])
         + jax.nn.sigmoid(gl[:, d:2 * d]) * _dot(yb_ref[0], wob_ref[...])
         + jax.nn.sigmoid(gl[:, 2 * d:3 * d]) * _dot(z_ref[0], wpw_ref[...]))
    y_s[...] = _dot(m.astype(BF16), wout_ref[...])
    mod = mod_ref[0]
    x1 = _layer_norm(DEEPNORM_ALPHA * x_ref[0] + mod[:, 0:d] * prev, g_ref[...], b_ref[...])
    x1_ref[0] = x1
    h2 = (x1 * (1.0 + mod[:, 2 * d:3 * d]) + mod[:, d:2 * d]).astype(BF16)
    h2_ref[0] = h2
    logits = _dot_nt(wrt_ref[...], h2)
    e = jnp.exp(logits - jnp.max(logits, axis=0, keepdims=True))
    aff_ref[0] = e / jnp.sum(e, axis=0, keepdims=True)


def _merge(x, ya, yb, z, gl, mod, mod_row, wl):
    bsz, n, d = x.shape
    tm = ROW_TILE
    tiles = n // tm
    n_tiles = bsz * tiles
    mm = lambda s: jnp.minimum(s, n_tiles - 1)
    fin = lambda s: jnp.maximum(s - 1, 0)
    if mod_row is None:
        mod_map = lambda s: (fin(s) // tiles, 0, 0)
    else:
        mod_map = lambda s: (mod_row, 0, 0)
    cur = lambda w: pl.BlockSpec((1, tm, w), lambda s: (mm(s) // tiles, mm(s) % tiles, 0))
    done = lambda w: pl.BlockSpec((1, tm, w), lambda s: (fin(s) // tiles, fin(s) % tiles, 0))
    const2 = lambda s: (0, 0)
    wspec = lambda r: pl.BlockSpec((r, d), const2)
    return pl.pallas_call(
        _merge_kernel,
        grid=(n_tiles + 1,),
        in_specs=[done(d), cur(NA_WIDTH), cur(NA_WIDTH), cur(CONV_DIM), cur(3 * d),
                  pl.BlockSpec((1, 1, 3 * d), mod_map),
                  wspec(NA_WIDTH), wspec(NA_WIDTH), wspec(CONV_DIM), wspec(d),
                  pl.BlockSpec((1, d), const2), pl.BlockSpec((1, d), const2),
                  pl.BlockSpec((N_EXPERTS, d), const2)],
        out_specs=[done(d), done(d),
                   pl.BlockSpec((1, N_EXPERTS, tm), lambda s: (fin(s) // tiles, 0, fin(s) % tiles))],
        out_shape=[jax.ShapeDtypeStruct((bsz, n, d), F32), jax.ShapeDtypeStruct((bsz, n, d), BF16),
                   jax.ShapeDtypeStruct((bsz, N_EXPERTS, n), F32)],
        scratch_shapes=[pltpu.VMEM((tm, d), F32)],
        compiler_params=_params("arbitrary"),
        name="merge_ln1_router",
    )(x, ya, yb, z, gl, mod, wl["w_oa"], wl["w_ob"], wl["w_pw2"], wl["w_out"], wl["ln1_g"], wl["ln1_b"], wl["w_rt"])


def _lane_cumsum(v):
    n = v.shape[1]
    lane = lax.broadcasted_iota(jnp.int32, v.shape, 1)
    s = 1
    while s < n:
        v = v + jnp.where(lane >= s, pltpu.roll(v, s, axis=1), 0)
        s *= 2
    return v


def _route_kernel(aff_ref, slot_ref, slot_t_ref, *, cap):
    nb, e, n = aff_ref.shape
    bits = pltpu.bitcast(aff_ref[...].reshape(nb * e, n), jnp.int32)

    def search(it, thr):
        cand = thr | lax.shift_left(jnp.int32(1), 30 - it)
        cnt = jnp.sum(jnp.where(bits >= cand, 1.0, 0.0), axis=1, keepdims=True)
        return jnp.where(cnt >= cap, cand, thr)

    thr = lax.fori_loop(0, 31, search, jnp.zeros((nb * e, 1), jnp.int32))
    gt = jnp.where(bits > thr, 1, 0)
    eq = jnp.where(bits == thr, 1, 0)
    inc = _lane_cumsum(gt + eq * 65536)
    gt_before = (inc & 65535) - gt
    eq_before = lax.shift_right_logical(inc, 16) - eq
    need = cap - jnp.sum(gt, axis=1, keepdims=True)
    sel = (gt == 1) | ((eq == 1) & (eq_before < need))
    slot = jnp.where(sel, gt_before + jnp.minimum(eq_before, need), -1)
    slot_ref[...] = slot.reshape(nb, e, n)
    filler = jnp.full((LANES - e, n), -1.0, F32)
    for j in range(nb):
        padded = jnp.concatenate([slot[j * e:(j + 1) * e, :].astype(F32), filler], axis=0)
        slot_t_ref[j] = padded.T.astype(jnp.int32)


def _route(aff_t, cap):
    bsz, e, n = aff_t.shape
    nb = min(bsz, max(1, ROUTE_ROWS // (e * n // LANES)))
    return pl.pallas_call(
        functools.partial(_route_kernel, cap=cap),
        grid=(bsz // nb,),
        in_specs=[pl.BlockSpec((nb, e, n), lambda b: (b, 0, 0))],
        out_specs=[pl.BlockSpec((nb, e, n), lambda b: (b, 0, 0)), pl.BlockSpec((nb, n, LANES), lambda b: (b, 0, 0))],
        out_shape=[jax.ShapeDtypeStruct((bsz, e, n), jnp.int32), jax.ShapeDtypeStruct((bsz, n, LANES), jnp.int32)],
        compiler_params=_params("parallel"),
        name="route",
    )(aff_t)


def _expert_kernel(slot_ref, aff_ref, h_ref, wg_ref, wu_ref, wd_ref, y_ref, wg_s, wu_s, wd_s, xg_s, ac_s, *, cap):
    @pl.when(pl.program_id(1) == 0)
    def _():
        wg_s[...] = wg_ref[0, 0].astype(BF16)
        wu_s[...] = wu_ref[0, 0].astype(BF16)
        wd_s[...] = wd_ref[0, 0].astype(BF16)

    nb, n = h_ref.shape[0], h_ref.shape[1]
    c_iota = lax.broadcasted_iota(jnp.int32, (cap, n), 0)
    for j in range(nb):
        hit = slot_ref[j, 0] == c_iota
        onehot = jnp.where(hit, 1.0, 0.0).astype(BF16)
        xg_s[j * cap:(j + 1) * cap, :] = _dot(onehot, h_ref[j]).astype(BF16)
        ac_s[j * cap:(j + 1) * cap, :] = jnp.sum(jnp.where(hit, aff_ref[j, 0], 0.0), axis=1, keepdims=True)
    xg = xg_s[...]
    hid = _silu(_dot(xg, wg_s[...])) * _dot(xg, wu_s[...])
    y = _dot(hid.astype(BF16), wd_s[...]) * ac_s[...]
    for j in range(nb):
        y_ref[j, 0] = y[j * cap:(j + 1) * cap, :].astype(BF16)


def _experts(slot, aff_t, h2, w_gate, w_up, w_down, layer, cap, nb):
    bsz, e, n = slot.shape
    d, f = w_gate.shape[2], w_gate.shape[3]
    slot4 = slot.reshape(bsz, e, 1, n)
    aff4 = aff_t.reshape(bsz, e, 1, n)
    return pl.pallas_call(
        functools.partial(_expert_kernel, cap=cap),
        grid=(e, bsz // nb),
        in_specs=[pl.BlockSpec((nb, 1, 1, n), lambda ei, bi: (bi, ei, 0, 0)),
                  pl.BlockSpec((nb, 1, 1, n), lambda ei, bi: (bi, ei, 0, 0)),
                  pl.BlockSpec((nb, n, d), lambda ei, bi: (bi, 0, 0)),
                  pl.BlockSpec((1, 1, d, f), lambda ei, bi: (layer, ei, 0, 0)),
                  pl.BlockSpec((1, 1, d, f), lambda ei, bi: (layer, ei, 0, 0)),
                  pl.BlockSpec((1, 1, f, d), lambda ei, bi: (layer, ei, 0, 0))],
        out_specs=pl.BlockSpec((nb, 1, cap, d), lambda ei, bi: (bi, ei, 0, 0)),
        out_shape=jax.ShapeDtypeStruct((bsz, e, cap, d), BF16),
        scratch_shapes=[pltpu.VMEM((d, f), BF16), pltpu.VMEM((d, f), BF16), pltpu.VMEM((f, d), BF16),
                        pltpu.VMEM((nb * cap, d), BF16), pltpu.VMEM((nb * cap, 1), F32)],
        compiler_params=_params("arbitrary", "arbitrary"),
        name="experts",
    )(slot4, aff4, h2, w_gate, w_up, w_down)


def _combine_kernel(st_ref, y_ref, x_ref, mod_ref, g_ref, b_ref, o_ref, moe_s, *, cap):
    s = pl.program_id(0)

    @pl.when(s == 0)
    def _():
        moe_s[...] = jnp.zeros(moe_s.shape, F32)

    prev = moe_s[...]
    st = st_ref[0]
    rows = st.shape[0]
    if cap % LANES == 0:
        c_iota = lax.broadcasted_iota(jnp.int32, (rows, cap), 1)
        scat = jnp.concatenate(
            [jnp.where(st[:, e:e + 1] == c_iota, 1.0, 0.0).astype(BF16) for e in range(N_EXPERTS)], axis=1)
    else:
        j_iota = lax.broadcasted_iota(jnp.int32, (rows, N_EXPERTS * cap), 1)
        hit = None
        for e in range(N_EXPERTS):
            col = st[:, e:e + 1]
            he = jnp.where(col >= 0, col + e * cap, -1) == j_iota
            hit = he if hit is None else (hit | he)
        scat = jnp.where(hit, 1.0, 0.0).astype(BF16)
    moe_s[...] = _dot(scat, y_ref[0])
    o_ref[0] = _layer_norm(DEEPNORM_ALPHA * x_ref[0] + mod_ref[0] * prev, g_ref[...], b_ref[...])


def _combine(slot_t, y, x1, mod, mod_row, wl, cap):
    bsz, n, d = x1.shape
    tm = ROW_TILE
    tiles = n // tm
    n_tiles = bsz * tiles
    mm = lambda s: jnp.minimum(s, n_tiles - 1)
    fin = lambda s: jnp.maximum(s - 1, 0)
    if mod_row is None:
        mod_map = lambda s: (fin(s) // tiles, 0, 0)
    else:
        mod_map = lambda s: (mod_row, 0, 0)
    ec = N_EXPERTS * cap
    return pl.pallas_call(
        functools.partial(_combine_kernel, cap=cap),
        grid=(n_tiles + 1,),
        in_specs=[pl.BlockSpec((1, tm, LANES), lambda s: (mm(s) // tiles, mm(s) % tiles, 0)),
                  pl.BlockSpec((1, ec, d), lambda s: (mm(s) // tiles, 0, 0)),
                  pl.BlockSpec((1, tm, d), lambda s: (fin(s) // tiles, fin(s) % tiles, 0)),
                  pl.BlockSpec((1, 1, d), mod_map),
                  pl.BlockSpec((1, d), lambda s: (0, 0)), pl.BlockSpec((1, d), lambda s: (0, 0))],
        out_specs=pl.BlockSpec((1, tm, d), lambda s: (fin(s) // tiles, fin(s) % tiles, 0)),
        out_shape=jax.ShapeDtypeStruct((bsz, n, d), F32),
        scratch_shapes=[pltpu.VMEM((tm, d), F32)],
        compiler_params=_params("arbitrary"),
        name="combine_ln2",
    )(slot_t, y.reshape(bsz, ec, d), x1, mod, wl["ln2_g"], wl["ln2_b"])


def _rope_tables(n_tokens):
    t = jnp.arange(n_tokens)
    row = (t // GRID_W).astype(F32)
    col = (t % GRID_W).astype(F32)
    n_freq = MLA_ROPE_DIM // 4
    inv_freq = ROPE_BASE ** (-jnp.arange(n_freq, dtype=F32) / n_freq)
    ar, ac = row[:, None] * inv_freq, col[:, None] * inv_freq
    z8 = jnp.zeros((n_tokens, n_freq), F32)
    ones = lambda w: jnp.ones((n_tokens, w), F32)
    zeros = lambda w: jnp.zeros((n_tokens, w), F32)
    cos_r, sin_r, cos_c, sin_c = jnp.cos(ar), jnp.sin(ar), jnp.cos(ac), jnp.sin(ac)
    tail = HEAD_LANES - MLA_NOPE_DIM - MLA_ROPE_DIM
    rc = jnp.concatenate([ones(MLA_NOPE_DIM), cos_r, cos_r, cos_c, cos_c, ones(tail)], axis=1)
    rsa = jnp.concatenate([zeros(MLA_NOPE_DIM), -sin_r, z8, -sin_c, z8, zeros(tail)], axis=1)
    rsb = jnp.concatenate([zeros(MLA_NOPE_DIM), z8, sin_r, z8, sin_c, zeros(tail)], axis=1)
    return rc, rsa, rsb


def _na_bias_table(rpb_l):
    col = jnp.arange(GRID_W)
    c_start = jnp.clip(col - NA_WIN_C // 2, 0, GRID_W - NA_WIN_C)
    col_mask = (col[None, :] >= c_start[:, None]) & (col[None, :] < c_start[:, None] + NA_WIN_C)
    dc_idx = jnp.clip(col[None, :] - col[:, None], -(NA_WIN_C - 1), NA_WIN_C - 1) + NA_WIN_C - 1
    blocks = jnp.where(col_mask[None, None], rpb_l[:, :, dc_idx] * LOG2_E, NEG_BIG)
    blocks = jnp.swapaxes(blocks, 2, 3)
    blocks = jnp.pad(blocks, ((0, 0), (NA_TAB_LO + 1, NA_TAB_HI), (0, 0), (0, 0)), constant_values=NEG_BIG)
    pairs = jnp.concatenate([blocks[:, 1:], blocks[:, :-1]], axis=-1)
    return pairs.reshape(NA_HEADS * NA_TAB, GRID_W, 2 * GRID_W).astype(F32)


def _prep_layer(l, w_in, g_q, w_uq, g_kv, w_ukv, rpb, w_dw, b_dw, g_cn, b_cn, w_pw2, w_oa, w_ob, w_out,
                ln1_g, ln1_b, w_router, ln2_g, ln2_b):
    d = D_MODEL
    wi = w_in[l]
    off = [0, 512, 1024, 1536, 1792, 1920, 1952, 2976, 6048]
    zpad = lambda w: jnp.zeros((d, w), F32)
    w_main = jnp.concatenate([
        wi[:, off[0]:off[5]],
        zpad(MLA_NOPE_DIM), wi[:, off[5]:off[6]], zpad(HEAD_LANES - MLA_NOPE_DIM - MLA_ROPE_DIM),
        wi[:, off[6]:off[8]],
    ], axis=1).astype(BF16)
    hq = MLA_NOPE_DIM + MLA_ROPE_DIM
    wq = w_uq[l].reshape(MLA_Q_RANK, MLA_HEADS, hq)
    wq = jnp.pad(wq, ((0, 0), (0, 0), (0, HEAD_LANES - hq))).reshape(MLA_Q_RANK, MLA_HEADS * HEAD_LANES)
    wkv = w_ukv[l].reshape(MLA_KV_RANK, MLA_HEADS, MLA_NOPE_DIM + MLA_V_DIM)
    wk = jnp.pad(wkv[:, :, :MLA_NOPE_DIM], ((0, 0), (0, 0), (0, HEAD_LANES - MLA_NOPE_DIM)))
    wv = wkv[:, :, MLA_NOPE_DIM:]
    return dict(
        w_in=w_main, g_q=g_q[l][None], g_kv=g_kv[l][None],
        w_uq=wq.astype(BF16), w_uk=wk.reshape(MLA_KV_RANK, MLA_HEADS * HEAD_LANES).astype(BF16),
        w_uv=wv.reshape(MLA_KV_RANK, MLA_HEADS * MLA_V_DIM).astype(BF16),
        na_tab=_na_bias_table(rpb[l]),
        w_dw=w_dw[l], b_dw=b_dw[l][None], g_cn=g_cn[l][None], b_cn=b_cn[l][None],
        w_pw2=w_pw2[l].astype(BF16), w_oa=w_oa[l].astype(BF16), w_ob=w_ob[l].astype(BF16), w_out=w_out[l].astype(BF16),
        ln1_g=ln1_g[l][None], ln1_b=ln1_b[l][None], w_rt=w_router[l].T.astype(BF16),
        ln2_g=ln2_g[l][None], ln2_b=ln2_b[l][None])


def _moe(x1, h2, aff_t, mod_g2, mod_row, wl, w_gate, w_up, w_down, layer, nb):
    n = x1.shape[1]
    cap = EC_CAPACITY_FACTOR * n // N_EXPERTS
    slot, slot_t = _route(aff_t, cap)
    y = _experts(slot, aff_t, h2, w_gate, w_up, w_down, layer, cap, nb)
    return _combine(slot_t, y, x1, mod_g2, mod_row, wl, cap)


def kernel(x, c, ctx, c_ctx, w_ada, b_ada, w_in, g_q, w_uq, g_kv, w_ukv, rpb, w_dw, b_dw, g_cn, b_cn, w_pw2, w_oa,
           w_ob, w_out, ln1_g, ln1_b, w_router, w_gate, w_up, w_down, ln2_g, ln2_b):
    bsz, s, d = x.shape
    depth = w_ada.shape[0]
    ctx_row = bsz
    n_rows = -(-(bsz + 1) // SUBLANES) * SUBLANES
    cc = jnp.concatenate([c, c_ctx[None], jnp.zeros((n_rows - bsz - 1, d), F32)], axis=0)
    mod_all = _ada_all_layers(cc, w_ada, b_ada)
    rope = _rope_tables(s)
    for l in range(depth):
        last = l == depth - 1
        wl = _prep_layer(l, w_in, g_q, w_uq, g_kv, w_ukv, rpb, w_dw, b_dw, g_cn, b_cn, w_pw2, w_oa, w_ob, w_out,
                         ln1_g, ln1_b, w_router, ln2_g, ln2_b)
        mod = mod_all[l][:, None, :]
        mod_a, mod_e, mod_h = mod[:, :, 0:2 * d], mod[:, :, 2 * d:5 * d], mod[:, :, 5 * d:6 * d]
        qa, ka, va, qb, kb, vb, z, gl = _in_proj(x, mod_a, None, wl, rope, True)
        if last:
            ka_c, va_c, kb_c, vb_c = _in_proj(ctx, mod_a, ctx_row, wl, None, False)
        else:
            qa_c, ka_c, va_c, qb_c, kb_c, vb_c, z_c, gl_c = _in_proj(ctx, mod_a, ctx_row, wl, None, True)
        ya = _na_attention(qa, ka, va, ka_c, va_c, wl["na_tab"])
        yb = _mla_attention(qb, [(kb_c, vb_c), (kb, vb)])
        x1, h2, aff_t = _merge(x, ya, yb, z, gl, mod_e, None, wl)
        x = _moe(x1, h2, aff_t, mod_h, None, wl, w_gate, w_up, w_down, l, 1)
        if not last:
            ya_c = _mla_attention(qa_c, [(ka_c, va_c)], paired=True)
            yb_c = _mla_attention(qb_c, [(kb_c, vb_c)])
            c1, h2_c, aff_c = _merge(ctx, ya_c, yb_c, z_c, gl_c, mod_e, ctx_row, wl)
            ctx = _moe(c1, h2_c, aff_c, mod_h, ctx_row, wl, w_gate, w_up, w_down, l, bsz)
    return x
```

```python
import functools

import jax
import jax.numpy as jnp
from jax import lax
from jax.experimental import pallas as pl
from jax.experimental.pallas import tpu as pltpu

F32 = jnp.float32
BF16 = jnp.bfloat16

D_MODEL = 1024
DEPTH = 4
GRID_W = 64
NA_HEADS = 8
NA_HEAD_DIM = 64
NA_WIN_R = 8
NA_WIN_C = 16
NA_WIDTH = NA_HEADS * NA_HEAD_DIM
MLA_HEADS = 8
MLA_Q_RANK = 256
MLA_KV_RANK = 128
MLA_NOPE_DIM = 64
MLA_ROPE_DIM = 32
MLA_V_DIM = 64
ROPE_BASE = 10000.0
CONV_DIM = 512
CONV_WIDTH = 31
N_EXPERTS = 16
EXPERT_DIM = 1024
EC_CAPACITY_FACTOR = 2
LN_EPS = 1e-5
RMS_EPS = 1e-6
DEEPNORM_ALPHA = (2 * DEPTH) ** 0.25

LANES = 128
SUBLANES = 8
VMEM_LIMIT_BYTES = 56 * 1024 * 1024

ROW_TILE = 256
NA_Q_ROWS = 4
NA_BAND_ROWS = NA_Q_ROWS + NA_WIN_R
ROUTE_ROWS = 1024
CONV_CHUNK = 64
CONV_HALO = 16
HEAD_LANES = 128
MLA_KEY_CHUNK = 4096
MLA_Q_TILE = 1024
NEG_BIG = -1e30
LOG2_E = 1.4426950408889634

C_QA, C_KA, C_VA = 0, 512, 1024
C_CQ, C_CKV, C_KR = 1536, 1792, 1920
C_CONV_A, C_CONV_G, C_GATES = 2048, 2560, 3072
IN_COLS_PAD = 6144
NA_TAB_LO = NA_BAND_ROWS - NA_WIN_R
NA_TAB_HI = NA_BAND_ROWS - NA_WIN_R
NA_TAB = 2 * NA_WIN_R - 1 + NA_TAB_LO + NA_TAB_HI


def _params(*sem):
    return pltpu.CompilerParams(dimension_semantics=sem, vmem_limit_bytes=VMEM_LIMIT_BYTES)


def _layer_spec(shape, layer, **kw):
    zeros = (0,) * len(shape)
    return pl.BlockSpec((None,) + tuple(shape), lambda *g: (layer,) + zeros, **kw)


def _dot(a, b):
    return jnp.dot(a, b, preferred_element_type=F32)


def _dot_nt(a, b):
    return lax.dot_general(a, b, (((1,), (1,)), ((), ())), preferred_element_type=F32)


def _layer_norm(x, g, b):
    mu = jnp.mean(x, axis=-1, keepdims=True)
    var = jnp.mean(jnp.square(x - mu), axis=-1, keepdims=True)
    return (x - mu) * lax.rsqrt(var + LN_EPS) * g + b


def _rms_norm(x, g):
    return x * lax.rsqrt(jnp.mean(jnp.square(x), axis=-1, keepdims=True) + RMS_EPS) * g


def _silu(x):
    return x * jax.nn.sigmoid(x)


def _ada_kernel(c_ref, w_ref, b_ref, o_ref):
    s = _silu(c_ref[...]).astype(BF16)
    o_ref[0] = _dot(s, w_ref[0].astype(BF16)) + b_ref[0]


def _ada_all_layers(cc, w_ada, b_ada):
    n_layers, d, cols = w_ada.shape
    rows = cc.shape[0]
    tn = 1536
    return pl.pallas_call(
        _ada_kernel,
        grid=(n_layers, cols // tn),
        in_specs=[pl.BlockSpec((rows, d), lambda l, j: (0, 0)),
                  pl.BlockSpec((1, d, tn), lambda l, j: (l, 0, j)),
                  pl.BlockSpec((1, 1, tn), lambda l, j: (l, 0, j))],
        out_specs=pl.BlockSpec((1, rows, tn), lambda l, j: (l, 0, j)),
        out_shape=jax.ShapeDtypeStruct((n_layers, rows, cols), F32),
        compiler_params=_params("arbitrary", "arbitrary"),
        name="ada_ln",
    )(cc, w_ada, b_ada.reshape(n_layers, 1, cols))


def _rope(x, c, sa, sb):
    return x * c + pltpu.roll(x, HEAD_LANES - 8, axis=1) * sa + pltpu.roll(x, 8, axis=1) * sb


def _conv_taps(u_ext, w_ref):
    half = CONV_WIDTH // 2
    rows = CONV_CHUNK + 2 * CONV_HALO
    out = []
    for t0 in range(0, u_ext.shape[0] - 2 * CONV_HALO, CONV_CHUNK):
        accs = []
        for cb in range(CONV_DIM // LANES):
            ls = slice(cb * LANES, (cb + 1) * LANES)
            win = u_ext[t0:t0 + rows, ls]
            a = jnp.zeros((CONV_CHUNK, LANES), F32)
            for r in range(SUBLANES):
                ph = win if r == 0 else pltpu.roll(win, rows - r, axis=0)
                for lo in range(r, CONV_HALO - half + CONV_WIDTH, SUBLANES):
                    k = lo - (CONV_HALO - half)
                    if k >= 0:
                        a = a + ph[lo - r:lo - r + CONV_CHUNK, :] * w_ref[k:k + 1, ls]
            accs.append(a)
        out.append(jnp.concatenate(accs, axis=1))
    return jnp.concatenate(out, axis=0)


def _store_values_t(vt_ref, v):
    v_t = v.T
    ones = jnp.ones((NA_HEAD_DIM, v_t.shape[1]), BF16)
    for hd in range(v_t.shape[0] // NA_HEAD_DIM):
        vt_ref[0, hd * HEAD_LANES:hd * HEAD_LANES + NA_HEAD_DIM, :] = v_t[hd * NA_HEAD_DIM:(hd + 1) * NA_HEAD_DIM, :].astype(BF16)
        vt_ref[0, hd * HEAD_LANES + NA_HEAD_DIM:(hd + 1) * HEAD_LANES, :] = ones


def _in_proj_kernel(*refs, use_rope, full):
    n_in = 8 + (3 if use_rope else 0) + (6 if full else 0)
    ins, outs = refs[:n_in], refs[n_in:]
    x_ref, mod_ref, w_ref, gq_ref, gkv_ref, wuq_ref, wuk_ref, wuv_ref = ins[:8]
    pos = 8
    if use_rope:
        rc_ref, rsa_ref, rsb_ref = ins[pos:pos + 3]
        pos += 3
    if full:
        xp_ref, xn_ref, wdw_ref, bdw_ref, gcn_ref, bcn_ref = ins[pos:pos + 6]
        qa_ref, ka_ref, va_ref, qb_ref, kb_ref, vb_ref, z_ref, gl_ref = outs
    else:
        ka_ref, va_ref, kb_ref, vb_ref = outs
    d = D_MODEL
    mod = mod_ref[0]
    modulate = lambda xv: (xv * (1.0 + mod[:, d:2 * d]) + mod[:, 0:d]).astype(BF16)
    h = modulate(x_ref[0])

    def proj(c0, width):
        return _dot(h, w_ref[:, c0:c0 + width])

    if full:
        i = pl.program_id(1)
        h_ext = jnp.concatenate([modulate(xp_ref[0]), h, modulate(xn_ref[0])], axis=0)
        u_ext = (_dot(h_ext, w_ref[:, C_CONV_A:C_CONV_A + CONV_DIM])
                 * jax.nn.sigmoid(_dot(h_ext, w_ref[:, C_CONV_G:C_CONV_G + CONV_DIM])))
        row = lax.broadcasted_iota(jnp.int32, u_ext.shape, 0)
        inside = ((row >= CONV_HALO) | (i > 0)) & ((row < u_ext.shape[0] - CONV_HALO) | (i < pl.num_programs(1) - 1))
        u_ext = jnp.where(inside, u_ext, 0.0)
        for t0 in range(0, h.shape[0], CONV_CHUNK):
            conv = _conv_taps(u_ext[t0:t0 + CONV_CHUNK + 2 * CONV_HALO, :], wdw_ref)
            z_ref[0, t0:t0 + CONV_CHUNK, :] = _silu(_layer_norm(conv + bdw_ref[...], gcn_ref[...], bcn_ref[...])).astype(BF16)
    ka_ref[0] = proj(C_KA, NA_WIDTH).astype(BF16)
    _store_values_t(va_ref, proj(C_VA, NA_WIDTH))
    if use_rope:
        rc, rsa, rsb = rc_ref[...], rsa_ref[...], rsb_ref[...]
    n_kv = _rms_norm(proj(C_CKV, MLA_KV_RANK), gkv_ref[...]).astype(BF16)
    kr = proj(C_KR, HEAD_LANES)
    if use_rope:
        kr = _rope(kr, rc, rsa, rsb)
    k_nope = _dot(n_kv, wuk_ref[...])
    for hd in range(MLA_HEADS):
        sl = slice(hd * HEAD_LANES, (hd + 1) * HEAD_LANES)
        kb_ref[0, :, sl] = (k_nope[:, sl] + kr).astype(BF16)
    _store_values_t(vb_ref, _dot(n_kv, wuv_ref[...]))
    if not full:
        return
    qa_ref[0] = (proj(C_QA, NA_WIDTH) * (LOG2_E * NA_HEAD_DIM ** -0.5)).astype(BF16)
    n_q = _rms_norm(proj(C_CQ, MLA_Q_RANK), gq_ref[...]).astype(BF16)
    q = _dot(n_q, wuq_ref[...])
    mla_scale = LOG2_E * (MLA_NOPE_DIM + MLA_ROPE_DIM) ** -0.5
    for hd in range(MLA_HEADS):
        sl = slice(hd * HEAD_LANES, (hd + 1) * HEAD_LANES)
        qh = q[:, sl]
        if use_rope:
            qh = _rope(qh, rc, rsa, rsb)
        qb_ref[0, :, sl] = (qh * mla_scale).astype(BF16)
    gl_ref[0] = jax.nn.sigmoid(proj(C_GATES, 3 * d)).astype(BF16)


def _in_proj(x, mod_all, mod_row, wl, layer, rope, full):
    bsz, n, d = x.shape
    tm = ROW_TILE
    use_rope = rope is not None
    if mod_row is None:
        mod_map = lambda b, i: (layer, b, 0, 0)
    else:
        mod_map = lambda b, i: (layer, mod_row, 0, 0)
    const2 = lambda b, i: (0, 0)
    lspec = lambda shape, **kw: _layer_spec(shape, layer, **kw)
    row_spec = lambda w: pl.BlockSpec((1, tm, w), lambda b, i: (b, i, 0))
    in_specs = [row_spec(d),
                pl.BlockSpec((None, 1, 1, 2 * d), mod_map),
                lspec((d, IN_COLS_PAD), pipeline_mode=pl.Buffered(1)),
                lspec((1, MLA_Q_RANK)),
                lspec((1, MLA_KV_RANK)),
                lspec((MLA_Q_RANK, MLA_HEADS * HEAD_LANES)),
                lspec((MLA_KV_RANK, MLA_HEADS * HEAD_LANES)),
                lspec((MLA_KV_RANK, MLA_HEADS * MLA_V_DIM))]
    args = [x, mod_all, wl["w_in"], wl["g_q"], wl["g_kv"], wl["w_uq"], wl["w_uk"], wl["w_uv"]]
    if use_rope:
        in_specs += [pl.BlockSpec((tm, HEAD_LANES), lambda b, i: (i, 0))] * 3
        args += list(rope)
    if full:
        per_tile = tm // CONV_HALO
        last_halo = n // CONV_HALO - 1
        vec = lspec((1, CONV_DIM))
        in_specs += [pl.BlockSpec((1, CONV_HALO, d), lambda b, i: (b, jnp.maximum(i * per_tile - 1, 0), 0)),
                     pl.BlockSpec((1, CONV_HALO, d), lambda b, i: (b, jnp.minimum((i + 1) * per_tile, last_halo), 0)),
                     lspec((CONV_WIDTH, CONV_DIM)), vec, vec, vec]
        args += [x, x, wl["w_dw"], wl["b_dw"], wl["g_cn"], wl["b_cn"]]
    sd = lambda w, dt: jax.ShapeDtypeStruct((bsz, n, w), dt)
    vt_shape = jax.ShapeDtypeStruct((bsz, MLA_HEADS * HEAD_LANES, n), BF16)
    vt_spec = pl.BlockSpec((1, MLA_HEADS * HEAD_LANES, tm), lambda b, i: (b, 0, i))
    kv_shapes = [sd(NA_WIDTH, BF16), vt_shape, sd(MLA_HEADS * HEAD_LANES, BF16), vt_shape]
    kv_specs = [row_spec(NA_WIDTH), vt_spec, row_spec(MLA_HEADS * HEAD_LANES), vt_spec]
    if full:
        out_shape = [sd(NA_WIDTH, BF16), kv_shapes[0], kv_shapes[1], sd(MLA_HEADS * HEAD_LANES, BF16), kv_shapes[2],
                     kv_shapes[3], sd(CONV_DIM, BF16), sd(3 * d, BF16)]
        out_specs = [row_spec(NA_WIDTH), kv_specs[0], kv_specs[1], row_spec(MLA_HEADS * HEAD_LANES), kv_specs[2],
                     kv_specs[3], row_spec(CONV_DIM), row_spec(3 * d)]
    else:
        out_shape, out_specs = kv_shapes, kv_specs
    return pl.pallas_call(
        functools.partial(_in_proj_kernel, use_rope=use_rope, full=full),
        grid=(bsz, n // tm),
        in_specs=in_specs,
        out_specs=out_specs,
        out_shape=out_shape,
        compiler_params=_params("parallel", "arbitrary"),
        name="in_proj",
    )(*args)


def _pair_lane_mask(rows):
    return lax.broadcasted_iota(jnp.int32, (rows, LANES), 1) < NA_HEAD_DIM


def _na_kernel(q_ref, k_ref, vt0_ref, vt1_ref, vt2_ref, kc_ref, vct_ref, tab_ref, o_ref, *, grid_rows):
    i = pl.program_id(1)
    tq = NA_Q_ROWS * GRID_W
    nband = NA_BAND_ROWS * GRID_W
    r0 = jnp.clip(NA_Q_ROWS * i - NA_WIN_R // 2, 0, grid_rows - NA_BAND_ROWS)
    kstart = pl.multiple_of(r0 * GRID_W, GRID_W)
    krow = r0 + lax.broadcasted_iota(jnp.int32, (nband, tq), 0) // GRID_W
    qrow = NA_Q_ROWS * i + lax.broadcasted_iota(jnp.int32, (nband, tq), 1) // GRID_W
    rstart = jnp.clip(qrow - NA_WIN_R // 2, 0, grid_rows - NA_WIN_R)
    row_mask = jnp.where((krow >= rstart) & (krow < rstart + NA_WIN_R), 0.0, NEG_BIG)
    tab_off = r0 - NA_Q_ROWS * i + (NA_WIN_R - 1) + NA_TAB_LO
    first = _pair_lane_mask(tq)
    vt_refs = (vt0_ref, vt1_ref, vt2_ref)
    vt_cols = vt0_ref.shape[2]

    def scores_of(hd):
        p, hh = divmod(hd, 2)
        ls = slice(p * LANES, (p + 1) * LANES)
        qp = q_ref[0, :, ls]
        qm = jnp.where(first if hh == 0 else jnp.logical_not(first), qp, jnp.zeros_like(qp))
        return _dot_nt(k_ref[0, pl.ds(kstart, nband), ls], qm), _dot_nt(kc_ref[0, :, ls], qm)

    def finish(hd, sw, sc):
        hs = slice(hd * HEAD_LANES, (hd + 1) * HEAD_LANES)
        bias = jnp.concatenate(
            [jnp.concatenate([tab_ref[hd * NA_TAB + tab_off + j - 2 * a2] for a2 in range(NA_Q_ROWS // 2)], axis=1)
             for j in range(NA_BAND_ROWS)], axis=0)
        sw = sw + (bias + row_mask)
        mx = jnp.maximum(jnp.max(sw, axis=0, keepdims=True), jnp.max(sc, axis=0, keepdims=True))
        pw = jnp.exp2(sw - mx).astype(BF16)
        o_t = _dot(vct_ref[0, hs, :], jnp.exp2(sc - mx).astype(BF16))
        for m, vt_ref in enumerate(vt_refs):
            o_t = o_t + _dot(vt_ref[0, hs, :], pw[m * vt_cols:(m + 1) * vt_cols, :])
        return o_t[0:NA_HEAD_DIM, :] / o_t[NA_HEAD_DIM:HEAD_LANES, :]

    outs = []
    nxt = scores_of(0)
    for hd in range(NA_HEADS):
        cur = nxt
        if hd + 1 < NA_HEADS:
            nxt = scores_of(hd + 1)
        outs.append(finish(hd, *cur))
    for p in range(NA_HEADS // 2):
        o_ref[0, :, p * LANES:(p + 1) * LANES] = jnp.concatenate(outs[2 * p:2 * p + 2], axis=0).T.astype(BF16)


def _na_attention(qa, ka, va_t, ka_c, va_t_c, tab, layer):
    bsz, s, w = qa.shape
    n_ctx = ka_c.shape[1]
    grid_rows = s // GRID_W
    assert grid_rows >= NA_BAND_ROWS and grid_rows % NA_Q_ROWS == 0 and NA_BAND_ROWS % NA_Q_ROWS == 0
    tq = NA_Q_ROWS * GRID_W
    n_vt = NA_BAND_ROWS // NA_Q_ROWS
    last_blk = grid_rows // NA_Q_ROWS - n_vt

    def vt_spec(m):
        return pl.BlockSpec((1, va_t.shape[1], tq), lambda b, i: (b, 0, jnp.clip(i - 1, 0, last_blk) + m))

    assert NA_WIN_R // 2 == NA_Q_ROWS and n_vt == 3
    return pl.pallas_call(
        functools.partial(_na_kernel, grid_rows=grid_rows),
        grid=(bsz, s // tq),
        in_specs=[pl.BlockSpec((1, tq, w), lambda b, i: (b, i, 0)),
                  pl.BlockSpec((1, s, w), lambda b, i: (b, 0, 0)),
                  vt_spec(0), vt_spec(1), vt_spec(2),
                  pl.BlockSpec((1, n_ctx, w), lambda b, i: (b, 0, 0)),
                  pl.BlockSpec((1, va_t_c.shape[1], n_ctx), lambda b, i: (b, 0, 0)),
                  _layer_spec(tab.shape[1:], layer, pipeline_mode=pl.Buffered(1))],
        out_specs=pl.BlockSpec((1, tq, w), lambda b, i: (b, i, 0)),
        out_shape=jax.ShapeDtypeStruct((bsz, s, w), BF16),
        compiler_params=_params("parallel", "arbitrary"),
        name="na_attn",
    )(qa, ka, va_t, va_t, va_t, ka_c, va_t_c, tab)


def _mla_attn_kernel(*refs, n_seg, paired):
    q_ref, kv_refs, o_ref = refs[0], refs[1:1 + 2 * n_seg], refs[1 + 2 * n_seg]
    first = _pair_lane_mask(q_ref.shape[1])
    chunks = []
    for j in range(n_seg):
        nk = kv_refs[2 * j].shape[1]
        step = nk if nk <= MLA_KEY_CHUNK else MLA_KEY_CHUNK
        chunks += [(j, c0, step) for c0 in range(0, nk, step)]

    def scores_of(hd):
        hs = slice(hd * HEAD_LANES, (hd + 1) * HEAD_LANES)
        if paired:
            hs = slice((hd // 2) * LANES, (hd // 2 + 1) * LANES)
            qp = q_ref[0, :, hs]
            qh = jnp.where(first if hd % 2 == 0 else jnp.logical_not(first), qp, jnp.zeros_like(qp))
        else:
            qh = q_ref[0, :, hs]
        return [_dot_nt(kv_refs[2 * j][0, c0:c0 + sz, hs], qh) for j, c0, sz in chunks]

    def finish(hd, scores):
        hs = slice(hd * HEAD_LANES, (hd + 1) * HEAD_LANES)
        mx = functools.reduce(jnp.maximum, [jnp.max(sj, axis=0, keepdims=True) for sj in scores])
        o_t = functools.reduce(lambda a, b: a + b,
                               [_dot(kv_refs[2 * j + 1][0, hs, c0:c0 + sz], jnp.exp2(sj - mx).astype(BF16))
                                for (j, c0, sz), sj in zip(chunks, scores)])
        return o_t[0:MLA_V_DIM, :] / o_t[MLA_V_DIM:HEAD_LANES, :]

    outs = []
    nxt = scores_of(0)
    for hd in range(MLA_HEADS):
        cur = nxt
        if hd + 1 < MLA_HEADS:
            nxt = scores_of(hd + 1)
        outs.append(finish(hd, cur))
    for p in range(MLA_HEADS // 2):
        o_ref[0, :, p * LANES:(p + 1) * LANES] = jnp.concatenate(outs[2 * p:2 * p + 2], axis=0).T.astype(BF16)


def _mla_attention(q, kvs, paired=False):
    bsz, nq, wq = q.shape
    tq = min(nq, MLA_Q_TILE)
    in_specs = [pl.BlockSpec((1, tq, wq), lambda b, i: (b, i, 0))]
    args = [q]
    for k, v_t in kvs:
        in_specs += [pl.BlockSpec((1, k.shape[1], k.shape[2]), lambda b, i: (b, 0, 0)),
                     pl.BlockSpec((1, v_t.shape[1], v_t.shape[2]), lambda b, i: (b, 0, 0))]
        args += [k, v_t]
    return pl.pallas_call(
        functools.partial(_mla_attn_kernel, n_seg=len(kvs), paired=paired),
        grid=(bsz, nq // tq),
        in_specs=in_specs,
        out_specs=pl.BlockSpec((1, tq, MLA_HEADS * MLA_V_DIM), lambda b, i: (b, i, 0)),
        out_shape=jax.ShapeDtypeStruct((bsz, nq, MLA_HEADS * MLA_V_DIM), BF16),
        compiler_params=_params("parallel", "arbitrary"),
        name="mla_attn",
    )(*args)


def _merge_kernel(x_ref, ya_ref, yb_ref, z_ref, gl_ref, g1_ref, sh2_ref, sc2_ref, woa_ref, wob_ref, wpw_ref, wout_ref,
                  g_ref, b_ref, wrt_ref, x1_ref, h2_ref, aff_ref, y_s):
    d = D_MODEL
    s = pl.program_id(0)

    @pl.when(s == 0)
    def _():
        y_s[...] = jnp.zeros(y_s.shape, F32)

    prev = y_s[...]
    gate = lambda br: gl_ref[0, :, br * d:(br + 1) * d].astype(F32)
    m = (gate(0) * _dot(ya_ref[0], woa_ref[...]) + gate(1) * _dot(yb_ref[0], wob_ref[...])
         + gate(2) * _dot(z_ref[0], wpw_ref[...]))
    y_s[...] = _dot(m.astype(BF16), wout_ref[...])
    x1 = _layer_norm(DEEPNORM_ALPHA * x_ref[0] + g1_ref[0] * prev, g_ref[...], b_ref[...])
    x1_ref[0] = x1
    h2 = (x1 * (1.0 + sc2_ref[0]) + sh2_ref[0]).astype(BF16)
    h2_ref[0] = h2
    logits = _dot_nt(wrt_ref[...], h2)
    e = jnp.exp(logits - jnp.max(logits, axis=0, keepdims=True))
    aff_ref[0] = e / jnp.sum(e, axis=0, keepdims=True)


def _merge(x, ya, yb, z, gl, mod_all, mod_row, wl, layer):
    bsz, n, d = x.shape
    tm = ROW_TILE
    tiles = n // tm
    n_tiles = bsz * tiles
    mm = lambda s: jnp.minimum(s, n_tiles - 1)
    fin = lambda s: jnp.maximum(s - 1, 0)
    row_of = (lambda s: fin(s) // tiles) if mod_row is None else (lambda s: mod_row)
    mod_col = lambda j: pl.BlockSpec((None, 1, 1, d), lambda s: (layer, row_of(s), 0, j))
    cur = lambda w: pl.BlockSpec((1, tm, w), lambda s: (mm(s) // tiles, mm(s) % tiles, 0))
    done = lambda w: pl.BlockSpec((1, tm, w), lambda s: (fin(s) // tiles, fin(s) % tiles, 0))
    lspec = lambda shape: _layer_spec(shape, layer)
    return pl.pallas_call(
        _merge_kernel,
        grid=(n_tiles + 1,),
        in_specs=[done(d), cur(NA_WIDTH), cur(NA_WIDTH), cur(CONV_DIM), cur(3 * d),
                  mod_col(2), mod_col(3), mod_col(4),
                  lspec((NA_WIDTH, d)), lspec((NA_WIDTH, d)), lspec((CONV_DIM, d)), lspec((d, d)),
                  lspec((1, d)), lspec((1, d)), lspec((N_EXPERTS, d))],
        out_specs=[done(d), done(d),
                   pl.BlockSpec((1, N_EXPERTS, tm), lambda s: (fin(s) // tiles, 0, fin(s) % tiles))],
        out_shape=[jax.ShapeDtypeStruct((bsz, n, d), F32), jax.ShapeDtypeStruct((bsz, n, d), BF16),
                   jax.ShapeDtypeStruct((bsz, N_EXPERTS, n), F32)],
        scratch_shapes=[pltpu.VMEM((tm, d), F32)],
        compiler_params=_params("arbitrary"),
        name="merge_ln1_router",
    )(x, ya, yb, z, gl, mod_all, mod_all, mod_all, wl["w_oa"], wl["w_ob"], wl["w_pw2"], wl["w_out"],
      wl["ln1_g"], wl["ln1_b"], wl["w_rt"])


def _lane_cumsum(v):
    n = v.shape[1]
    lane = lax.broadcasted_iota(jnp.int32, v.shape, 1)
    s = 1
    while s < n:
        v = v + jnp.where(lane >= s, pltpu.roll(v, s, axis=1), 0)
        s *= 2
    return v


def _route_kernel(aff_ref, slot_ref, slot_t_ref, *, cap):
    nb, e, n = aff_ref.shape
    bits = pltpu.bitcast(aff_ref[...].reshape(nb * e, n), jnp.int32)

    def search(it, thr):
        cand = thr | lax.shift_left(jnp.int32(1), 30 - it)
        cnt = jnp.sum(jnp.where(bits >= cand, 1.0, 0.0), axis=1, keepdims=True)
        return jnp.where(cnt >= cap, cand, thr)

    thr = lax.fori_loop(0, 31, search, jnp.zeros((nb * e, 1), jnp.int32))
    gt = jnp.where(bits > thr, 1, 0)
    eq = jnp.where(bits == thr, 1, 0)
    inc = _lane_cumsum(gt + eq * 65536)
    gt_before = (inc & 65535) - gt
    eq_before = lax.shift_right_logical(inc, 16) - eq
    need = cap - jnp.sum(gt, axis=1, keepdims=True)
    sel = (gt == 1) | ((eq == 1) & (eq_before < need))
    slot = jnp.where(sel, gt_before + jnp.minimum(eq_before, need), -1)
    slot_ref[...] = slot.reshape(nb, e, n)
    filler = jnp.full((LANES - e, n), -1.0, F32)
    for j in range(nb):
        padded = jnp.concatenate([slot[j * e:(j + 1) * e, :].astype(F32), filler], axis=0)
        slot_t_ref[j] = padded.T.astype(jnp.int32)


def _route(aff_t, cap):
    bsz, e, n = aff_t.shape
    nb = min(bsz, max(1, ROUTE_ROWS // (e * n // LANES)))
    return pl.pallas_call(
        functools.partial(_route_kernel, cap=cap),
        grid=(bsz // nb,),
        in_specs=[pl.BlockSpec((nb, e, n), lambda b: (b, 0, 0))],
        out_specs=[pl.BlockSpec((nb, e, n), lambda b: (b, 0, 0)), pl.BlockSpec((nb, n, LANES), lambda b: (b, 0, 0))],
        out_shape=[jax.ShapeDtypeStruct((bsz, e, n), jnp.int32), jax.ShapeDtypeStruct((bsz, n, LANES), jnp.int32)],
        compiler_params=_params("parallel"),
        name="route",
    )(aff_t)


def _expert_kernel(slot_ref, aff_ref, h_ref, wg_ref, wu_ref, wd_ref, y_ref, wg_s, wu_s, wd_s, xg_s, ac_s, *, cap):
    @pl.when(pl.program_id(1) == 0)
    def _():
        wg_s[...] = wg_ref[0, 0].astype(BF16)
        wu_s[...] = wu_ref[0, 0].astype(BF16)
        wd_s[...] = wd_ref[0, 0].astype(BF16)

    nb, n = h_ref.shape[0], h_ref.shape[1]
    c_iota = lax.broadcasted_iota(jnp.int32, (cap, n), 0)
    for j in range(nb):
        hit = slot_ref[j, 0] == c_iota
        onehot = jnp.where(hit, 1.0, 0.0).astype(BF16)
        xg_s[j * cap:(j + 1) * cap, :] = _dot(onehot, h_ref[j]).astype(BF16)
        ac_s[j * cap:(j + 1) * cap, :] = jnp.sum(jnp.where(hit, aff_ref[j, 0], 0.0), axis=1, keepdims=True)
    xg = xg_s[...]
    hid = _silu(_dot(xg, wg_s[...])) * _dot(xg, wu_s[...])
    y = _dot(hid.astype(BF16), wd_s[...]) * ac_s[...]
    for j in range(nb):
        y_ref[j, 0] = y[j * cap:(j + 1) * cap, :].astype(BF16)


def _experts(slot, aff_t, h2, w_gate, w_up, w_down, layer, cap, nb):
    bsz, e, n = slot.shape
    d, f = w_gate.shape[2], w_gate.shape[3]
    slot4 = slot.reshape(bsz, e, 1, n)
    aff4 = aff_t.reshape(bsz, e, 1, n)
    return pl.pallas_call(
        functools.partial(_expert_kernel, cap=cap),
        grid=(e, bsz // nb),
        in_specs=[pl.BlockSpec((nb, 1, 1, n), lambda ei, bi: (bi, ei, 0, 0)),
                  pl.BlockSpec((nb, 1, 1, n), lambda ei, bi: (bi, ei, 0, 0)),
                  pl.BlockSpec((nb, n, d), lambda ei, bi: (bi, 0, 0)),
                  pl.BlockSpec((1, 1, d, f), lambda ei, bi: (layer, ei, 0, 0)),
                  pl.BlockSpec((1, 1, d, f), lambda ei, bi: (layer, ei, 0, 0)),
                  pl.BlockSpec((1, 1, f, d), lambda ei, bi: (layer, ei, 0, 0))],
        out_specs=pl.BlockSpec((nb, 1, cap, d), lambda ei, bi: (bi, ei, 0, 0)),
        out_shape=jax.ShapeDtypeStruct((bsz, e, cap, d), BF16),
        scratch_shapes=[pltpu.VMEM((d, f), BF16), pltpu.VMEM((d, f), BF16), pltpu.VMEM((f, d), BF16),
                        pltpu.VMEM((nb * cap, d), BF16), pltpu.VMEM((nb * cap, 1), F32)],
        compiler_params=_params("arbitrary", "arbitrary"),
        name="experts",
    )(slot4, aff4, h2, w_gate, w_up, w_down)


def _combine_kernel(st_ref, y_ref, x_ref, mod_ref, g_ref, b_ref, o_ref, moe_s, *, cap):
    s = pl.program_id(0)

    @pl.when(s == 0)
    def _():
        moe_s[...] = jnp.zeros(moe_s.shape, F32)

    prev = moe_s[...]
    st = st_ref[0]
    rows = st.shape[0]
    if cap % LANES == 0:
        c_iota = lax.broadcasted_iota(jnp.int32, (rows, cap), 1)
        scat = jnp.concatenate(
            [jnp.where(st[:, e:e + 1] == c_iota, 1.0, 0.0).astype(BF16) for e in range(N_EXPERTS)], axis=1)
    else:
        j_iota = lax.broadcasted_iota(jnp.int32, (rows, N_EXPERTS * cap), 1)
        hit = None
        for e in range(N_EXPERTS):
            col = st[:, e:e + 1]
            he = jnp.where(col >= 0, col + e * cap, -1) == j_iota
            hit = he if hit is None else (hit | he)
        scat = jnp.where(hit, 1.0, 0.0).astype(BF16)
    moe_s[...] = _dot(scat, y_ref[0])
    o_ref[0] = _layer_norm(DEEPNORM_ALPHA * x_ref[0] + mod_ref[0] * prev, g_ref[...], b_ref[...])


def _combine(slot_t, y, x1, mod_all, mod_row, wl, layer, cap):
    bsz, n, d = x1.shape
    tm = ROW_TILE
    tiles = n // tm
    n_tiles = bsz * tiles
    mm = lambda s: jnp.minimum(s, n_tiles - 1)
    fin = lambda s: jnp.maximum(s - 1, 0)
    row_of = (lambda s: fin(s) // tiles) if mod_row is None else (lambda s: mod_row)
    ec = N_EXPERTS * cap
    return pl.pallas_call(
        functools.partial(_combine_kernel, cap=cap),
        grid=(n_tiles + 1,),
        in_specs=[pl.BlockSpec((1, tm, LANES), lambda s: (mm(s) // tiles, mm(s) % tiles, 0)),
                  pl.BlockSpec((1, ec, d), lambda s: (mm(s) // tiles, 0, 0)),
                  pl.BlockSpec((1, tm, d), lambda s: (fin(s) // tiles, fin(s) % tiles, 0)),
                  pl.BlockSpec((None, 1, 1, d), lambda s: (layer, row_of(s), 0, 5)),
                  _layer_spec((1, d), layer), _layer_spec((1, d), layer)],
        out_specs=pl.BlockSpec((1, tm, d), lambda s: (fin(s) // tiles, fin(s) % tiles, 0)),
        out_shape=jax.ShapeDtypeStruct((bsz, n, d), F32),
        scratch_shapes=[pltpu.VMEM((tm, d), F32)],
        compiler_params=_params("arbitrary"),
        name="combine_ln2",
    )(slot_t, y.reshape(bsz, ec, d), x1, mod_all, wl["ln2_g"], wl["ln2_b"])


def _rope_tables(n_tokens):
    t = jnp.arange(n_tokens)
    row = (t // GRID_W).astype(F32)
    col = (t % GRID_W).astype(F32)
    n_freq = MLA_ROPE_DIM // 4
    inv_freq = ROPE_BASE ** (-jnp.arange(n_freq, dtype=F32) / n_freq)
    ar, ac = row[:, None] * inv_freq, col[:, None] * inv_freq
    z8 = jnp.zeros((n_tokens, n_freq), F32)
    ones = lambda w: jnp.ones((n_tokens, w), F32)
    zeros = lambda w: jnp.zeros((n_tokens, w), F32)
    cos_r, sin_r, cos_c, sin_c = jnp.cos(ar), jnp.sin(ar), jnp.cos(ac), jnp.sin(ac)
    tail = HEAD_LANES - MLA_NOPE_DIM - MLA_ROPE_DIM
    rc = jnp.concatenate([ones(MLA_NOPE_DIM), cos_r, cos_r, cos_c, cos_c, ones(tail)], axis=1)
    rsa = jnp.concatenate([zeros(MLA_NOPE_DIM), -sin_r, z8, -sin_c, z8, zeros(tail)], axis=1)
    rsb = jnp.concatenate([zeros(MLA_NOPE_DIM), z8, sin_r, z8, sin_c, zeros(tail)], axis=1)
    return rc, rsa, rsb


def _na_bias_table(rpb):
    col = jnp.arange(GRID_W)
    c_start = jnp.clip(col - NA_WIN_C // 2, 0, GRID_W - NA_WIN_C)
    col_mask = (col[None, :] >= c_start[:, None]) & (col[None, :] < c_start[:, None] + NA_WIN_C)
    dc_idx = jnp.clip(col[None, :] - col[:, None], -(NA_WIN_C - 1), NA_WIN_C - 1) + NA_WIN_C - 1
    blocks = jnp.where(col_mask, rpb[..., dc_idx] * LOG2_E, NEG_BIG)
    blocks = jnp.swapaxes(blocks, -1, -2)
    blocks = jnp.pad(blocks, ((0, 0), (0, 0), (NA_TAB_LO + 1, NA_TAB_HI), (0, 0), (0, 0)), constant_values=NEG_BIG)
    pairs = jnp.concatenate([blocks[:, :, 1:], blocks[:, :, :-1]], axis=-1)
    return pairs.reshape(rpb.shape[0], NA_HEADS * NA_TAB, GRID_W, 2 * GRID_W).astype(F32)


def _prep_params(w_in, g_q, w_uq, g_kv, w_ukv, rpb, w_dw, b_dw, g_cn, b_cn, w_pw2, w_oa, w_ob, w_out,
                 ln1_g, ln1_b, w_router, ln2_g, ln2_b):
    n_layers, d = w_in.shape[0], D_MODEL
    off = [0, 512, 1024, 1536, 1792, 1920, 1952, 2976, 6048]
    zpad = lambda w: jnp.zeros((n_layers, d, w), F32)
    w_main = jnp.concatenate([
        w_in[:, :, off[0]:off[5]],
        zpad(MLA_NOPE_DIM), w_in[:, :, off[5]:off[6]], zpad(HEAD_LANES - MLA_NOPE_DIM - MLA_ROPE_DIM),
        w_in[:, :, off[6]:off[8]],
    ], axis=2).astype(BF16)
    hq = MLA_NOPE_DIM + MLA_ROPE_DIM
    wq = w_uq.reshape(n_layers, MLA_Q_RANK, MLA_HEADS, hq)
    wq = jnp.pad(wq, ((0, 0), (0, 0), (0, 0), (0, HEAD_LANES - hq))).reshape(n_layers, MLA_Q_RANK, MLA_HEADS * HEAD_LANES)
    wkv = w_ukv.reshape(n_layers, MLA_KV_RANK, MLA_HEADS, MLA_NOPE_DIM + MLA_V_DIM)
    wk = jnp.pad(wkv[..., :MLA_NOPE_DIM], ((0, 0), (0, 0), (0, 0), (0, HEAD_LANES - MLA_NOPE_DIM)))
    wv = wkv[..., MLA_NOPE_DIM:]
    vec = lambda v: v[:, None, :]
    return dict(
        w_in=w_main, g_q=vec(g_q), g_kv=vec(g_kv),
        w_uq=wq.astype(BF16), w_uk=wk.reshape(n_layers, MLA_KV_RANK, MLA_HEADS * HEAD_LANES).astype(BF16),
        w_uv=wv.reshape(n_layers, MLA_KV_RANK, MLA_HEADS * MLA_V_DIM).astype(BF16),
        na_tab=_na_bias_table(rpb),
        w_dw=w_dw, b_dw=vec(b_dw), g_cn=vec(g_cn), b_cn=vec(b_cn),
        w_pw2=w_pw2.astype(BF16), w_oa=w_oa.astype(BF16), w_ob=w_ob.astype(BF16), w_out=w_out.astype(BF16),
        ln1_g=vec(ln1_g), ln1_b=vec(ln1_b), w_rt=jnp.swapaxes(w_router, 1, 2).astype(BF16),
        ln2_g=vec(ln2_g), ln2_b=vec(ln2_b))


def _moe(x1, h2, aff_t, mod_all, mod_row, wl, w_gate, w_up, w_down, layer, nb):
    n = x1.shape[1]
    cap = EC_CAPACITY_FACTOR * n // N_EXPERTS
    slot, slot_t = _route(aff_t, cap)
    y = _experts(slot, aff_t, h2, w_gate, w_up, w_down, layer, cap, nb)
    return _combine(slot_t, y, x1, mod_all, mod_row, wl, layer, cap)


def kernel(x, c, ctx, c_ctx, w_ada, b_ada, w_in, g_q, w_uq, g_kv, w_ukv, rpb, w_dw, b_dw, g_cn, b_cn, w_pw2, w_oa,
           w_ob, w_out, ln1_g, ln1_b, w_router, w_gate, w_up, w_down, ln2_g, ln2_b):
    bsz, s, d = x.shape
    depth = w_ada.shape[0]
    ctx_row = bsz
    n_rows = -(-(bsz + 1) // SUBLANES) * SUBLANES
    cc = jnp.concatenate([c, c_ctx[None], jnp.zeros((n_rows - bsz - 1, d), F32)], axis=0)
    mod_all = _ada_all_layers(cc, w_ada, b_ada).reshape(depth, n_rows, 1, 6 * d)
    rope = _rope_tables(s)
    wl = _prep_params(w_in, g_q, w_uq, g_kv, w_ukv, rpb, w_dw, b_dw, g_cn, b_cn, w_pw2, w_oa, w_ob, w_out,
                      ln1_g, ln1_b, w_router, ln2_g, ln2_b)
    for l in range(depth):
        last = l == depth - 1
        qa, ka, va, qb, kb, vb, z, gl = _in_proj(x, mod_all, None, wl, l, rope, True)
        if last:
            ka_c, va_c, kb_c, vb_c = _in_proj(ctx, mod_all, ctx_row, wl, l, None, False)
        else:
            qa_c, ka_c, va_c, qb_c, kb_c, vb_c, z_c, gl_c = _in_proj(ctx, mod_all, ctx_row, wl, l, None, True)
        ya = _na_attention(qa, ka, va, ka_c, va_c, wl["na_tab"], l)
        yb = _mla_attention(qb, [(kb_c, vb_c), (kb, vb)])
        x1, h2, aff_t = _merge(x, ya, yb, z, gl, mod_all, None, wl, l)
        x = _moe(x1, h2, aff_t, mod_all, None, wl, w_gate, w_up, w_down, l, 1)
        if not last:
            ya_c = _mla_attention(qa_c, [(ka_c, va_c)], paired=True)
            yb_c = _mla_attention(qb_c, [(kb_c, vb_c)])
            c1, h2_c, aff_c = _merge(ctx, ya_c, yb_c, z_c, gl_c, mod_all, ctx_row, wl, l)
            ctx = _moe(c1, h2_c, aff_c, mod_all, ctx_row, wl, w_gate, w_up, w_down, l, bsz)
    return x
```

```python
import functools

import jax
import jax.numpy as jnp
from jax import lax
from jax.experimental import pallas as pl
from jax.experimental.pallas import tpu as pltpu

F32 = jnp.float32
BF16 = jnp.bfloat16

D_MODEL = 1024
DEPTH = 4
GRID_W = 64
NA_HEADS = 8
NA_HEAD_DIM = 64
NA_WIN_R = 8
NA_WIN_C = 16
NA_WIDTH = NA_HEADS * NA_HEAD_DIM
MLA_HEADS = 8
MLA_Q_RANK = 256
MLA_KV_RANK = 128
MLA_NOPE_DIM = 64
MLA_ROPE_DIM = 32
MLA_V_DIM = 64
ROPE_BASE = 10000.0
CONV_DIM = 512
CONV_WIDTH = 31
N_EXPERTS = 16
EXPERT_DIM = 1024
EC_CAPACITY_FACTOR = 2
LN_EPS = 1e-5
RMS_EPS = 1e-6
DEEPNORM_ALPHA = (2 * DEPTH) ** 0.25

LANES = 128
SUBLANES = 8
VMEM_LIMIT_BYTES = 56 * 1024 * 1024

ROW_TILE = 512
NA_Q_ROWS = 4
NA_BAND_ROWS = NA_Q_ROWS + NA_WIN_R
ROUTE_ROWS = 1024
CONV_CHUNK = 64
CONV_HALO = 16
HEAD_LANES = 128
MLA_KEY_CHUNK = 4096
MLA_Q_TILE = 1024
NEG_BIG = -1e30
LOG2_E = 1.4426950408889634

C_QA, C_KA, C_VA = 0, 512, 1024
C_CQ, C_CKV, C_KR = 1536, 1792, 1920
C_CONV_A, C_CONV_G, C_GATES = 2048, 2560, 3072
IN_COLS_PAD = 6144
NA_TAB_LO = NA_BAND_ROWS - NA_WIN_R
NA_TAB_HI = NA_BAND_ROWS - NA_WIN_R
NA_TAB = 2 * NA_WIN_R - 1 + NA_TAB_LO + NA_TAB_HI


def _params(*sem):
    return pltpu.CompilerParams(dimension_semantics=sem, vmem_limit_bytes=VMEM_LIMIT_BYTES)


def _layer_spec(shape, layer, **kw):
    zeros = (0,) * len(shape)
    return pl.BlockSpec((None,) + tuple(shape), lambda *g: (layer,) + zeros, **kw)


def _dot(a, b):
    return jnp.dot(a, b, preferred_element_type=F32)


def _dot_nt(a, b):
    return lax.dot_general(a, b, (((1,), (1,)), ((), ())), preferred_element_type=F32)


def _layer_norm(x, g, b):
    mu = jnp.mean(x, axis=-1, keepdims=True)
    var = jnp.mean(jnp.square(x - mu), axis=-1, keepdims=True)
    return (x - mu) * lax.rsqrt(var + LN_EPS) * g + b


def _rms_norm(x, g):
    return x * lax.rsqrt(jnp.mean(jnp.square(x), axis=-1, keepdims=True) + RMS_EPS) * g


def _silu(x):
    return x * jax.nn.sigmoid(x)


def _ada_kernel(c_ref, w_ref, b_ref, o_ref):
    s = _silu(c_ref[...]).astype(BF16)
    o_ref[0] = _dot(s, w_ref[0].astype(BF16)) + b_ref[0]


def _ada_all_layers(cc, w_ada, b_ada):
    n_layers, d, cols = w_ada.shape
    rows = cc.shape[0]
    tn = 1536
    return pl.pallas_call(
        _ada_kernel,
        grid=(n_layers, cols // tn),
        in_specs=[pl.BlockSpec((rows, d), lambda l, j: (0, 0)),
                  pl.BlockSpec((1, d, tn), lambda l, j: (l, 0, j)),
                  pl.BlockSpec((1, 1, tn), lambda l, j: (l, 0, j))],
        out_specs=pl.BlockSpec((1, rows, tn), lambda l, j: (l, 0, j)),
        out_shape=jax.ShapeDtypeStruct((n_layers, rows, cols), F32),
        compiler_params=_params("arbitrary", "arbitrary"),
        name="ada_ln",
    )(cc, w_ada, b_ada.reshape(n_layers, 1, cols))


def _rope(x, c, sa, sb):
    return x * c + pltpu.roll(x, HEAD_LANES - 8, axis=1) * sa + pltpu.roll(x, 8, axis=1) * sb


def _conv_taps(u_ext, w_ref):
    half = CONV_WIDTH // 2
    rows = CONV_CHUNK + 2 * CONV_HALO
    out = []
    for t0 in range(0, u_ext.shape[0] - 2 * CONV_HALO, CONV_CHUNK):
        accs = []
        for cb in range(CONV_DIM // LANES):
            ls = slice(cb * LANES, (cb + 1) * LANES)
            win = u_ext[t0:t0 + rows, ls]
            a = jnp.zeros((CONV_CHUNK, LANES), F32)
            for r in range(SUBLANES):
                ph = win if r == 0 else pltpu.roll(win, rows - r, axis=0)
                for lo in range(r, CONV_HALO - half + CONV_WIDTH, SUBLANES):
                    k = lo - (CONV_HALO - half)
                    if k >= 0:
                        a = a + ph[lo - r:lo - r + CONV_CHUNK, :] * w_ref[k:k + 1, ls]
            accs.append(a)
        out.append(jnp.concatenate(accs, axis=1))
    return jnp.concatenate(out, axis=0)


def _store_values_t(vt_ref, v):
    v_t = v.T
    ones = jnp.ones((NA_HEAD_DIM, v_t.shape[1]), BF16)
    for hd in range(v_t.shape[0] // NA_HEAD_DIM):
        vt_ref[0, hd * HEAD_LANES:hd * HEAD_LANES + NA_HEAD_DIM, :] = v_t[hd * NA_HEAD_DIM:(hd + 1) * NA_HEAD_DIM, :].astype(BF16)
        vt_ref[0, hd * HEAD_LANES + NA_HEAD_DIM:(hd + 1) * HEAD_LANES, :] = ones


def _in_proj_kernel(*refs, use_rope, full):
    n_in = 8 + (3 if use_rope else 0) + (6 if full else 0)
    ins, outs = refs[:n_in], refs[n_in:]
    x_ref, mod_ref, w_ref, gq_ref, gkv_ref, wuq_ref, wuk_ref, wuv_ref = ins[:8]
    pos = 8
    if use_rope:
        rc_ref, rsa_ref, rsb_ref = ins[pos:pos + 3]
        pos += 3
    if full:
        xp_ref, xn_ref, wdw_ref, bdw_ref, gcn_ref, bcn_ref = ins[pos:pos + 6]
        qa_ref, ka_ref, va_ref, qb_ref, kb_ref, vb_ref, z_ref, gl_ref = outs
    else:
        ka_ref, va_ref, kb_ref, vb_ref = outs
    d = D_MODEL
    mod = mod_ref[0]
    modulate = lambda xv: (xv * (1.0 + mod[:, d:2 * d]) + mod[:, 0:d]).astype(BF16)
    h = modulate(x_ref[0])

    def proj(c0, width):
        return _dot(h, w_ref[:, c0:c0 + width])

    if full:
        i = pl.program_id(1)
        h_ext = jnp.concatenate([modulate(xp_ref[0]), h, modulate(xn_ref[0])], axis=0)
        u_ext = (_dot(h_ext, w_ref[:, C_CONV_A:C_CONV_A + CONV_DIM])
                 * jax.nn.sigmoid(_dot(h_ext, w_ref[:, C_CONV_G:C_CONV_G + CONV_DIM])))
        row = lax.broadcasted_iota(jnp.int32, u_ext.shape, 0)
        inside = ((row >= CONV_HALO) | (i > 0)) & ((row < u_ext.shape[0] - CONV_HALO) | (i < pl.num_programs(1) - 1))
        u_ext = jnp.where(inside, u_ext, 0.0)
        for t0 in range(0, h.shape[0], CONV_CHUNK):
            conv = _conv_taps(u_ext[t0:t0 + CONV_CHUNK + 2 * CONV_HALO, :], wdw_ref)
            z_ref[0, t0:t0 + CONV_CHUNK, :] = _silu(_layer_norm(conv + bdw_ref[...], gcn_ref[...], bcn_ref[...])).astype(BF16)
    ka_ref[0] = proj(C_KA, NA_WIDTH).astype(BF16)
    _store_values_t(va_ref, proj(C_VA, NA_WIDTH))
    if use_rope:
        rc, rsa, rsb = rc_ref[...], rsa_ref[...], rsb_ref[...]
    n_kv = _rms_norm(proj(C_CKV, MLA_KV_RANK), gkv_ref[...]).astype(BF16)
    kr = proj(C_KR, HEAD_LANES)
    if use_rope:
        kr = _rope(kr, rc, rsa, rsb)
    k_nope = _dot(n_kv, wuk_ref[...])
    for hd in range(MLA_HEADS):
        sl = slice(hd * HEAD_LANES, (hd + 1) * HEAD_LANES)
        kb_ref[0, :, sl] = (k_nope[:, sl] + kr).astype(BF16)
    _store_values_t(vb_ref, _dot(n_kv, wuv_ref[...]))
    if not full:
        return
    qa_ref[0] = (proj(C_QA, NA_WIDTH) * (LOG2_E * NA_HEAD_DIM ** -0.5)).astype(BF16)
    n_q = _rms_norm(proj(C_CQ, MLA_Q_RANK), gq_ref[...]).astype(BF16)
    q = _dot(n_q, wuq_ref[...])
    mla_scale = LOG2_E * (MLA_NOPE_DIM + MLA_ROPE_DIM) ** -0.5
    for hd in range(MLA_HEADS):
        sl = slice(hd * HEAD_LANES, (hd + 1) * HEAD_LANES)
        qh = q[:, sl]
        if use_rope:
            qh = _rope(qh, rc, rsa, rsb)
        qb_ref[0, :, sl] = (qh * mla_scale).astype(BF16)
    gl_ref[0] = jax.nn.sigmoid(proj(C_GATES, 3 * d)).astype(BF16)


def _in_proj(x, mod_all, mod_row, wl, layer, rope, full):
    bsz, n, d = x.shape
    tm = min(n, ROW_TILE)
    use_rope = rope is not None
    if mod_row is None:
        mod_map = lambda b, i: (layer, b, 0, 0)
    else:
        mod_map = lambda b, i: (layer, mod_row, 0, 0)
    const2 = lambda b, i: (0, 0)
    lspec = lambda shape, **kw: _layer_spec(shape, layer, **kw)
    row_spec = lambda w: pl.BlockSpec((1, tm, w), lambda b, i: (b, i, 0))
    in_specs = [row_spec(d),
                pl.BlockSpec((None, 1, 1, 2 * d), mod_map),
                lspec((d, IN_COLS_PAD), pipeline_mode=pl.Buffered(1)),
                lspec((1, MLA_Q_RANK)),
                lspec((1, MLA_KV_RANK)),
                lspec((MLA_Q_RANK, MLA_HEADS * HEAD_LANES)),
                lspec((MLA_KV_RANK, MLA_HEADS * HEAD_LANES)),
                lspec((MLA_KV_RANK, MLA_HEADS * MLA_V_DIM))]
    args = [x, mod_all, wl["w_in"], wl["g_q"], wl["g_kv"], wl["w_uq"], wl["w_uk"], wl["w_uv"]]
    if use_rope:
        in_specs += [pl.BlockSpec((tm, HEAD_LANES), lambda b, i: (i, 0))] * 3
        args += list(rope)
    if full:
        per_tile = tm // CONV_HALO
        last_halo = n // CONV_HALO - 1
        vec = lspec((1, CONV_DIM))
        in_specs += [pl.BlockSpec((1, CONV_HALO, d), lambda b, i: (b, jnp.maximum(i * per_tile - 1, 0), 0)),
                     pl.BlockSpec((1, CONV_HALO, d), lambda b, i: (b, jnp.minimum((i + 1) * per_tile, last_halo), 0)),
                     lspec((CONV_WIDTH, CONV_DIM)), vec, vec, vec]
        args += [x, x, wl["w_dw"], wl["b_dw"], wl["g_cn"], wl["b_cn"]]
    sd = lambda w, dt: jax.ShapeDtypeStruct((bsz, n, w), dt)
    vt_shape = jax.ShapeDtypeStruct((bsz, MLA_HEADS * HEAD_LANES, n), BF16)
    vt_spec = pl.BlockSpec((1, MLA_HEADS * HEAD_LANES, tm), lambda b, i: (b, 0, i))
    kv_shapes = [sd(NA_WIDTH, BF16), vt_shape, sd(MLA_HEADS * HEAD_LANES, BF16), vt_shape]
    kv_specs = [row_spec(NA_WIDTH), vt_spec, row_spec(MLA_HEADS * HEAD_LANES), vt_spec]
    if full:
        out_shape = [sd(NA_WIDTH, BF16), kv_shapes[0], kv_shapes[1], sd(MLA_HEADS * HEAD_LANES, BF16), kv_shapes[2],
                     kv_shapes[3], sd(CONV_DIM, BF16), sd(3 * d, BF16)]
        out_specs = [row_spec(NA_WIDTH), kv_specs[0], kv_specs[1], row_spec(MLA_HEADS * HEAD_LANES), kv_specs[2],
                     kv_specs[3], row_spec(CONV_DIM), row_spec(3 * d)]
    else:
        out_shape, out_specs = kv_shapes, kv_specs
    return pl.pallas_call(
        functools.partial(_in_proj_kernel, use_rope=use_rope, full=full),
        grid=(bsz, n // tm),
        in_specs=in_specs,
        out_specs=out_specs,
        out_shape=out_shape,
        compiler_params=_params("parallel", "arbitrary"),
        name="in_proj",
    )(*args)


def _pair_lane_mask(rows):
    return lax.broadcasted_iota(jnp.int32, (rows, LANES), 1) < NA_HEAD_DIM


def _na_kernel(q_ref, k_ref, vt0_ref, vt1_ref, vt2_ref, kc_ref, vct_ref, tab_ref, o_ref, *, grid_rows):
    i = pl.program_id(1)
    tq = NA_Q_ROWS * GRID_W
    nband = NA_BAND_ROWS * GRID_W
    r0 = jnp.clip(NA_Q_ROWS * i - NA_WIN_R // 2, 0, grid_rows - NA_BAND_ROWS)
    kstart = pl.multiple_of(r0 * GRID_W, GRID_W)
    krow = r0 + lax.broadcasted_iota(jnp.int32, (nband, tq), 0) // GRID_W
    qrow = NA_Q_ROWS * i + lax.broadcasted_iota(jnp.int32, (nband, tq), 1) // GRID_W
    rstart = jnp.clip(qrow - NA_WIN_R // 2, 0, grid_rows - NA_WIN_R)
    row_mask = jnp.where((krow >= rstart) & (krow < rstart + NA_WIN_R), 0.0, NEG_BIG)
    tab_off = r0 - NA_Q_ROWS * i + (NA_WIN_R - 1) + NA_TAB_LO
    first = _pair_lane_mask(tq)
    vt_refs = (vt0_ref, vt1_ref, vt2_ref)
    vt_cols = vt0_ref.shape[2]

    def scores_of(hd):
        p, hh = divmod(hd, 2)
        ls = slice(p * LANES, (p + 1) * LANES)
        qp = q_ref[0, :, ls]
        qm = jnp.where(first if hh == 0 else jnp.logical_not(first), qp, jnp.zeros_like(qp))
        return _dot_nt(k_ref[0, pl.ds(kstart, nband), ls], qm), _dot_nt(kc_ref[0, :, ls], qm)

    def finish(hd, sw, sc):
        hs = slice(hd * HEAD_LANES, (hd + 1) * HEAD_LANES)
        bias = jnp.concatenate(
            [jnp.concatenate([tab_ref[hd * NA_TAB + tab_off + j - 2 * a2] for a2 in range(NA_Q_ROWS // 2)], axis=1)
             for j in range(NA_BAND_ROWS)], axis=0)
        sw = sw + (bias + row_mask)
        mx = jnp.maximum(jnp.max(sw, axis=0, keepdims=True), jnp.max(sc, axis=0, keepdims=True))
        pw = jnp.exp2(sw - mx).astype(BF16)
        o_t = _dot(vct_ref[0, hs, :], jnp.exp2(sc - mx).astype(BF16))
        for m, vt_ref in enumerate(vt_refs):
            o_t = o_t + _dot(vt_ref[0, hs, :], pw[m * vt_cols:(m + 1) * vt_cols, :])
        return o_t[0:NA_HEAD_DIM, :] / o_t[NA_HEAD_DIM:HEAD_LANES, :]

    outs = []
    nxt = scores_of(0)
    for hd in range(NA_HEADS):
        cur = nxt
        if hd + 1 < NA_HEADS:
            nxt = scores_of(hd + 1)
        outs.append(finish(hd, *cur))
    for p in range(NA_HEADS // 2):
        o_ref[0, :, p * LANES:(p + 1) * LANES] = jnp.concatenate(outs[2 * p:2 * p + 2], axis=0).T.astype(BF16)


def _na_attention(qa, ka, va_t, ka_c, va_t_c, tab, layer):
    bsz, s, w = qa.shape
    n_ctx = ka_c.shape[1]
    grid_rows = s // GRID_W
    assert grid_rows >= NA_BAND_ROWS and grid_rows % NA_Q_ROWS == 0 and NA_BAND_ROWS % NA_Q_ROWS == 0
    tq = NA_Q_ROWS * GRID_W
    n_vt = NA_BAND_ROWS // NA_Q_ROWS
    last_blk = grid_rows // NA_Q_ROWS - n_vt

    def vt_spec(m):
        return pl.BlockSpec((1, va_t.shape[1], tq), lambda b, i: (b, 0, jnp.clip(i - 1, 0, last_blk) + m))

    assert NA_WIN_R // 2 == NA_Q_ROWS and n_vt == 3
    return pl.pallas_call(
        functools.partial(_na_kernel, grid_rows=grid_rows),
        grid=(bsz, s // tq),
        in_specs=[pl.BlockSpec((1, tq, w), lambda b, i: (b, i, 0)),
                  pl.BlockSpec((1, s, w), lambda b, i: (b, 0, 0)),
                  vt_spec(0), vt_spec(1), vt_spec(2),
                  pl.BlockSpec((1, n_ctx, w), lambda b, i: (b, 0, 0)),
                  pl.BlockSpec((1, va_t_c.shape[1], n_ctx), lambda b, i: (b, 0, 0)),
                  _layer_spec(tab.shape[1:], layer, pipeline_mode=pl.Buffered(1))],
        out_specs=pl.BlockSpec((1, tq, w), lambda b, i: (b, i, 0)),
        out_shape=jax.ShapeDtypeStruct((bsz, s, w), BF16),
        compiler_params=_params("parallel", "arbitrary"),
        name="na_attn",
    )(qa, ka, va_t, va_t, va_t, ka_c, va_t_c, tab)


def _mla_attn_kernel(*refs, n_seg, paired):
    q_ref, kv_refs, o_ref = refs[0], refs[1:1 + 2 * n_seg], refs[1 + 2 * n_seg]
    first = _pair_lane_mask(q_ref.shape[1])
    chunks = []
    for j in range(n_seg):
        nk = kv_refs[2 * j].shape[1]
        step = nk if nk <= MLA_KEY_CHUNK else MLA_KEY_CHUNK
        chunks += [(j, c0, step) for c0 in range(0, nk, step)]

    def scores_of(hd):
        hs = slice(hd * HEAD_LANES, (hd + 1) * HEAD_LANES)
        if paired:
            hs = slice((hd // 2) * LANES, (hd // 2 + 1) * LANES)
            qp = q_ref[0, :, hs]
            qh = jnp.where(first if hd % 2 == 0 else jnp.logical_not(first), qp, jnp.zeros_like(qp))
        else:
            qh = q_ref[0, :, hs]
        return [_dot_nt(kv_refs[2 * j][0, c0:c0 + sz, hs], qh) for j, c0, sz in chunks]

    def finish(hd, scores):
        hs = slice(hd * HEAD_LANES, (hd + 1) * HEAD_LANES)
        mx = functools.reduce(jnp.maximum, [jnp.max(sj, axis=0, keepdims=True) for sj in scores])
        o_t = functools.reduce(lambda a, b: a + b,
                               [_dot(kv_refs[2 * j + 1][0, hs, c0:c0 + sz], jnp.exp2(sj - mx).astype(BF16))
                                for (j, c0, sz), sj in zip(chunks, scores)])
        return o_t[0:MLA_V_DIM, :] / o_t[MLA_V_DIM:HEAD_LANES, :]

    outs = []
    nxt = scores_of(0)
    for hd in range(MLA_HEADS):
        cur = nxt
        if hd + 1 < MLA_HEADS:
            nxt = scores_of(hd + 1)
        outs.append(finish(hd, cur))
    for p in range(MLA_HEADS // 2):
        o_ref[0, :, p * LANES:(p + 1) * LANES] = jnp.concatenate(outs[2 * p:2 * p + 2], axis=0).T.astype(BF16)


def _mla_attention(q, kvs, paired=False):
    bsz, nq, wq = q.shape
    tq = min(nq, MLA_Q_TILE)
    in_specs = [pl.BlockSpec((1, tq, wq), lambda b, i: (b, i, 0))]
    args = [q]
    for k, v_t in kvs:
        in_specs += [pl.BlockSpec((1, k.shape[1], k.shape[2]), lambda b, i: (b, 0, 0)),
                     pl.BlockSpec((1, v_t.shape[1], v_t.shape[2]), lambda b, i: (b, 0, 0))]
        args += [k, v_t]
    return pl.pallas_call(
        functools.partial(_mla_attn_kernel, n_seg=len(kvs), paired=paired),
        grid=(bsz, nq // tq),
        in_specs=in_specs,
        out_specs=pl.BlockSpec((1, tq, MLA_HEADS * MLA_V_DIM), lambda b, i: (b, i, 0)),
        out_shape=jax.ShapeDtypeStruct((bsz, nq, MLA_HEADS * MLA_V_DIM), BF16),
        compiler_params=_params("parallel", "arbitrary"),
        name="mla_attn",
    )(*args)


def _merge_kernel(x_ref, ya_ref, yb_ref, z_ref, gl_ref, g1_ref, sh2_ref, sc2_ref, woa_ref, wob_ref, wpw_ref, wout_ref,
                  g_ref, b_ref, wrt_ref, x1_ref, h2_ref, aff_ref, y_s):
    d = D_MODEL
    s = pl.program_id(0)

    @pl.when(s == 0)
    def _():
        y_s[...] = jnp.zeros(y_s.shape, F32)

    prev = y_s[...]
    gate = lambda br: gl_ref[0, :, br * d:(br + 1) * d].astype(F32)
    m = (gate(0) * _dot(ya_ref[0], woa_ref[...]) + gate(1) * _dot(yb_ref[0], wob_ref[...])
         + gate(2) * _dot(z_ref[0], wpw_ref[...]))
    y_s[...] = _dot(m.astype(BF16), wout_ref[...])
    x1 = _layer_norm(DEEPNORM_ALPHA * x_ref[0] + g1_ref[0] * prev, g_ref[...], b_ref[...])
    x1_ref[0] = x1
    h2 = (x1 * (1.0 + sc2_ref[0]) + sh2_ref[0]).astype(BF16)
    h2_ref[0] = h2
    logits = _dot_nt(wrt_ref[...], h2)
    e = jnp.exp(logits - jnp.max(logits, axis=0, keepdims=True))
    aff_ref[0] = e / jnp.sum(e, axis=0, keepdims=True)


def _merge(x, ya, yb, z, gl, mod_all, mod_row, wl, layer):
    bsz, n, d = x.shape
    tm = min(n, ROW_TILE)
    tiles = n // tm
    n_tiles = bsz * tiles
    mm = lambda s: jnp.minimum(s, n_tiles - 1)
    fin = lambda s: jnp.maximum(s - 1, 0)
    row_of = (lambda s: fin(s) // tiles) if mod_row is None else (lambda s: mod_row)
    mod_col = lambda j: pl.BlockSpec((None, 1, 1, d), lambda s: (layer, row_of(s), 0, j))
    cur = lambda w: pl.BlockSpec((1, tm, w), lambda s: (mm(s) // tiles, mm(s) % tiles, 0))
    done = lambda w: pl.BlockSpec((1, tm, w), lambda s: (fin(s) // tiles, fin(s) % tiles, 0))
    lspec = lambda shape: _layer_spec(shape, layer)
    return pl.pallas_call(
        _merge_kernel,
        grid=(n_tiles + 1,),
        in_specs=[done(d), cur(NA_WIDTH), cur(NA_WIDTH), cur(CONV_DIM), cur(3 * d),
                  mod_col(2), mod_col(3), mod_col(4),
                  lspec((NA_WIDTH, d)), lspec((NA_WIDTH, d)), lspec((CONV_DIM, d)), lspec((d, d)),
                  lspec((1, d)), lspec((1, d)), lspec((N_EXPERTS, d))],
        out_specs=[done(d), done(d),
                   pl.BlockSpec((1, N_EXPERTS, tm), lambda s: (fin(s) // tiles, 0, fin(s) % tiles))],
        out_shape=[jax.ShapeDtypeStruct((bsz, n, d), F32), jax.ShapeDtypeStruct((bsz, n, d), BF16),
                   jax.ShapeDtypeStruct((bsz, N_EXPERTS, n), F32)],
        scratch_shapes=[pltpu.VMEM((tm, d), F32)],
        compiler_params=_params("arbitrary"),
        name="merge_ln1_router",
    )(x, ya, yb, z, gl, mod_all, mod_all, mod_all, wl["w_oa"], wl["w_ob"], wl["w_pw2"], wl["w_out"],
      wl["ln1_g"], wl["ln1_b"], wl["w_rt"])


def _lane_cumsum(v):
    n = v.shape[1]
    lane = lax.broadcasted_iota(jnp.int32, v.shape, 1)
    s = 1
    while s < n:
        v = v + jnp.where(lane >= s, pltpu.roll(v, s, axis=1), 0)
        s *= 2
    return v


def _route_kernel(aff_ref, slot_ref, slot_t_ref, *, cap):
    nb, e, n = aff_ref.shape
    bits = pltpu.bitcast(aff_ref[...].reshape(nb * e, n), jnp.int32)

    def search(it, thr):
        cand = thr | lax.shift_left(jnp.int32(1), 30 - it)
        cnt = jnp.sum(jnp.where(bits >= cand, 1.0, 0.0), axis=1, keepdims=True)
        return jnp.where(cnt >= cap, cand, thr)

    thr = lax.fori_loop(0, 31, search, jnp.zeros((nb * e, 1), jnp.int32))
    gt = jnp.where(bits > thr, 1, 0)
    eq = jnp.where(bits == thr, 1, 0)
    inc = _lane_cumsum(gt + eq * 65536)
    gt_before = (inc & 65535) - gt
    eq_before = lax.shift_right_logical(inc, 16) - eq
    need = cap - jnp.sum(gt, axis=1, keepdims=True)
    sel = (gt == 1) | ((eq == 1) & (eq_before < need))
    slot = jnp.where(sel, gt_before + jnp.minimum(eq_before, need), -1)
    slot_ref[...] = slot.reshape(nb, e, n)
    filler = jnp.full((LANES - e, n), -1.0, F32)
    for j in range(nb):
        padded = jnp.concatenate([slot[j * e:(j + 1) * e, :].astype(F32), filler], axis=0)
        slot_t_ref[j] = padded.T.astype(jnp.int32)


def _route(aff_t, cap):
    bsz, e, n = aff_t.shape
    nb = min(bsz, max(1, ROUTE_ROWS // (e * n // LANES)))
    return pl.pallas_call(
        functools.partial(_route_kernel, cap=cap),
        grid=(bsz // nb,),
        in_specs=[pl.BlockSpec((nb, e, n), lambda b: (b, 0, 0))],
        out_specs=[pl.BlockSpec((nb, e, n), lambda b: (b, 0, 0)), pl.BlockSpec((nb, n, LANES), lambda b: (b, 0, 0))],
        out_shape=[jax.ShapeDtypeStruct((bsz, e, n), jnp.int32), jax.ShapeDtypeStruct((bsz, n, LANES), jnp.int32)],
        compiler_params=_params("parallel"),
        name="route",
    )(aff_t)


def _expert_kernel(slot_ref, aff_ref, h_ref, wg_ref, wu_ref, wd_ref, y_ref, wg_s, wu_s, wd_s, xg_s, ac_s, *, cap):
    @pl.when(pl.program_id(1) == 0)
    def _():
        wg_s[...] = wg_ref[0, 0].astype(BF16)
        wu_s[...] = wu_ref[0, 0].astype(BF16)
        wd_s[...] = wd_ref[0, 0].astype(BF16)

    nb, n = h_ref.shape[0], h_ref.shape[1]
    c_iota = lax.broadcasted_iota(jnp.int32, (cap, n), 0)
    for j in range(nb):
        hit = slot_ref[j, 0] == c_iota
        onehot = jnp.where(hit, 1.0, 0.0).astype(BF16)
        xg_s[j * cap:(j + 1) * cap, :] = _dot(onehot, h_ref[j]).astype(BF16)
        ac_s[j * cap:(j + 1) * cap, :] = jnp.sum(jnp.where(hit, aff_ref[j, 0], 0.0), axis=1, keepdims=True)
    xg = xg_s[...]
    hid = _silu(_dot(xg, wg_s[...])) * _dot(xg, wu_s[...])
    y = _dot(hid.astype(BF16), wd_s[...]) * ac_s[...]
    for j in range(nb):
        y_ref[j, 0] = y[j * cap:(j + 1) * cap, :].astype(BF16)


def _experts(slot, aff_t, h2, w_gate, w_up, w_down, layer, cap, nb):
    bsz, e, n = slot.shape
    d, f = w_gate.shape[2], w_gate.shape[3]
    slot4 = slot.reshape(bsz, e, 1, n)
    aff4 = aff_t.reshape(bsz, e, 1, n)
    return pl.pallas_call(
        functools.partial(_expert_kernel, cap=cap),
        grid=(e, bsz // nb),
        in_specs=[pl.BlockSpec((nb, 1, 1, n), lambda ei, bi: (bi, ei, 0, 0)),
                  pl.BlockSpec((nb, 1, 1, n), lambda ei, bi: (bi, ei, 0, 0)),
                  pl.BlockSpec((nb, n, d), lambda ei, bi: (bi, 0, 0)),
                  pl.BlockSpec((1, 1, d, f), lambda ei, bi: (layer, ei, 0, 0)),
                  pl.BlockSpec((1, 1, d, f), lambda ei, bi: (layer, ei, 0, 0)),
                  pl.BlockSpec((1, 1, f, d), lambda ei, bi: (layer, ei, 0, 0))],
        out_specs=pl.BlockSpec((nb, 1, cap, d), lambda ei, bi: (bi, ei, 0, 0)),
        out_shape=jax.ShapeDtypeStruct((bsz, e, cap, d), BF16),
        scratch_shapes=[pltpu.VMEM((d, f), BF16), pltpu.VMEM((d, f), BF16), pltpu.VMEM((f, d), BF16),
                        pltpu.VMEM((nb * cap, d), BF16), pltpu.VMEM((nb * cap, 1), F32)],
        compiler_params=_params("arbitrary", "arbitrary"),
        name="experts",
    )(slot4, aff4, h2, w_gate, w_up, w_down)


def _combine_kernel(st_ref, y_ref, x_ref, mod_ref, g_ref, b_ref, o_ref, moe_s, *, cap):
    s = pl.program_id(0)

    @pl.when(s == 0)
    def _():
        moe_s[...] = jnp.zeros(moe_s.shape, F32)

    prev = moe_s[...]
    st = st_ref[0]
    rows = st.shape[0]
    if cap % LANES == 0:
        c_iota = lax.broadcasted_iota(jnp.int32, (rows, cap), 1)
        scat = jnp.concatenate(
            [jnp.where(st[:, e:e + 1] == c_iota, 1.0, 0.0).astype(BF16) for e in range(N_EXPERTS)], axis=1)
    else:
        j_iota = lax.broadcasted_iota(jnp.int32, (rows, N_EXPERTS * cap), 1)
        hit = None
        for e in range(N_EXPERTS):
            col = st[:, e:e + 1]
            he = jnp.where(col >= 0, col + e * cap, -1) == j_iota
            hit = he if hit is None else (hit | he)
        scat = jnp.where(hit, 1.0, 0.0).astype(BF16)
    moe_s[...] = _dot(scat, y_ref[0])
    o_ref[0] = _layer_norm(DEEPNORM_ALPHA * x_ref[0] + mod_ref[0] * prev, g_ref[...], b_ref[...])


def _combine(slot_t, y, x1, mod_all, mod_row, wl, layer, cap):
    bsz, n, d = x1.shape
    tm = min(n, ROW_TILE)
    tiles = n // tm
    n_tiles = bsz * tiles
    mm = lambda s: jnp.minimum(s, n_tiles - 1)
    fin = lambda s: jnp.maximum(s - 1, 0)
    row_of = (lambda s: fin(s) // tiles) if mod_row is None else (lambda s: mod_row)
    ec = N_EXPERTS * cap
    return pl.pallas_call(
        functools.partial(_combine_kernel, cap=cap),
        grid=(n_tiles + 1,),
        in_specs=[pl.BlockSpec((1, tm, LANES), lambda s: (mm(s) // tiles, mm(s) % tiles, 0)),
                  pl.BlockSpec((1, ec, d), lambda s: (mm(s) // tiles, 0, 0)),
                  pl.BlockSpec((1, tm, d), lambda s: (fin(s) // tiles, fin(s) % tiles, 0)),
                  pl.BlockSpec((None, 1, 1, d), lambda s: (layer, row_of(s), 0, 5)),
                  _layer_spec((1, d), layer), _layer_spec((1, d), layer)],
        out_specs=pl.BlockSpec((1, tm, d), lambda s: (fin(s) // tiles, fin(s) % tiles, 0)),
        out_shape=jax.ShapeDtypeStruct((bsz, n, d), F32),
        scratch_shapes=[pltpu.VMEM((tm, d), F32)],
        compiler_params=_params("arbitrary"),
        name="combine_ln2",
    )(slot_t, y.reshape(bsz, ec, d), x1, mod_all, wl["ln2_g"], wl["ln2_b"])


def _rope_tables(n_tokens):
    t = jnp.arange(n_tokens)
    row = (t // GRID_W).astype(F32)
    col = (t % GRID_W).astype(F32)
    n_freq = MLA_ROPE_DIM // 4
    inv_freq = ROPE_BASE ** (-jnp.arange(n_freq, dtype=F32) / n_freq)
    ar, ac = row[:, None] * inv_freq, col[:, None] * inv_freq
    z8 = jnp.zeros((n_tokens, n_freq), F32)
    ones = lambda w: jnp.ones((n_tokens, w), F32)
    zeros = lambda w: jnp.zeros((n_tokens, w), F32)
    cos_r, sin_r, cos_c, sin_c = jnp.cos(ar), jnp.sin(ar), jnp.cos(ac), jnp.sin(ac)
    tail = HEAD_LANES - MLA_NOPE_DIM - MLA_ROPE_DIM
    rc = jnp.concatenate([ones(MLA_NOPE_DIM), cos_r, cos_r, cos_c, cos_c, ones(tail)], axis=1)
    rsa = jnp.concatenate([zeros(MLA_NOPE_DIM), -sin_r, z8, -sin_c, z8, zeros(tail)], axis=1)
    rsb = jnp.concatenate([zeros(MLA_NOPE_DIM), z8, sin_r, z8, sin_c, zeros(tail)], axis=1)
    return rc, rsa, rsb


def _na_bias_table(rpb):
    col = jnp.arange(GRID_W)
    c_start = jnp.clip(col - NA_WIN_C // 2, 0, GRID_W - NA_WIN_C)
    col_mask = (col[None, :] >= c_start[:, None]) & (col[None, :] < c_start[:, None] + NA_WIN_C)
    dc_idx = jnp.clip(col[None, :] - col[:, None], -(NA_WIN_C - 1), NA_WIN_C - 1) + NA_WIN_C - 1
    blocks = jnp.where(col_mask, rpb[..., dc_idx] * LOG2_E, NEG_BIG)
    blocks = jnp.swapaxes(blocks, -1, -2)
    blocks = jnp.pad(blocks, ((0, 0), (0, 0), (NA_TAB_LO + 1, NA_TAB_HI), (0, 0), (0, 0)), constant_values=NEG_BIG)
    pairs = jnp.concatenate([blocks[:, :, 1:], blocks[:, :, :-1]], axis=-1)
    return pairs.reshape(rpb.shape[0], NA_HEADS * NA_TAB, GRID_W, 2 * GRID_W).astype(F32)


def _prep_params(w_in, g_q, w_uq, g_kv, w_ukv, rpb, w_dw, b_dw, g_cn, b_cn, w_pw2, w_oa, w_ob, w_out,
                 ln1_g, ln1_b, w_router, ln2_g, ln2_b):
    n_layers, d = w_in.shape[0], D_MODEL
    off = [0, 512, 1024, 1536, 1792, 1920, 1952, 2976, 6048]
    zpad = lambda w: jnp.zeros((n_layers, d, w), F32)
    w_main = jnp.concatenate([
        w_in[:, :, off[0]:off[5]],
        zpad(MLA_NOPE_DIM), w_in[:, :, off[5]:off[6]], zpad(HEAD_LANES - MLA_NOPE_DIM - MLA_ROPE_DIM),
        w_in[:, :, off[6]:off[8]],
    ], axis=2).astype(BF16)
    hq = MLA_NOPE_DIM + MLA_ROPE_DIM
    wq = w_uq.reshape(n_layers, MLA_Q_RANK, MLA_HEADS, hq)
    wq = jnp.pad(wq, ((0, 0), (0, 0), (0, 0), (0, HEAD_LANES - hq))).reshape(n_layers, MLA_Q_RANK, MLA_HEADS * HEAD_LANES)
    wkv = w_ukv.reshape(n_layers, MLA_KV_RANK, MLA_HEADS, MLA_NOPE_DIM + MLA_V_DIM)
    wk = jnp.pad(wkv[..., :MLA_NOPE_DIM], ((0, 0), (0, 0), (0, 0), (0, HEAD_LANES - MLA_NOPE_DIM)))
    wv = wkv[..., MLA_NOPE_DIM:]
    vec = lambda v: v[:, None, :]
    return dict(
        w_in=w_main, g_q=vec(g_q), g_kv=vec(g_kv),
        w_uq=wq.astype(BF16), w_uk=wk.reshape(n_layers, MLA_KV_RANK, MLA_HEADS * HEAD_LANES).astype(BF16),
        w_uv=wv.reshape(n_layers, MLA_KV_RANK, MLA_HEADS * MLA_V_DIM).astype(BF16),
        na_tab=_na_bias_table(rpb),
        w_dw=w_dw, b_dw=vec(b_dw), g_cn=vec(g_cn), b_cn=vec(b_cn),
        w_pw2=w_pw2.astype(BF16), w_oa=w_oa.astype(BF16), w_ob=w_ob.astype(BF16), w_out=w_out.astype(BF16),
        ln1_g=vec(ln1_g), ln1_b=vec(ln1_b), w_rt=jnp.swapaxes(w_router, 1, 2).astype(BF16),
        ln2_g=vec(ln2_g), ln2_b=vec(ln2_b))


def _moe(x1, h2, aff_t, mod_all, mod_row, wl, w_gate, w_up, w_down, layer, nb):
    n = x1.shape[1]
    cap = EC_CAPACITY_FACTOR * n // N_EXPERTS
    slot, slot_t = _route(aff_t, cap)
    y = _experts(slot, aff_t, h2, w_gate, w_up, w_down, layer, cap, nb)
    return _combine(slot_t, y, x1, mod_all, mod_row, wl, layer, cap)


def kernel(x, c, ctx, c_ctx, w_ada, b_ada, w_in, g_q, w_uq, g_kv, w_ukv, rpb, w_dw, b_dw, g_cn, b_cn, w_pw2, w_oa,
           w_ob, w_out, ln1_g, ln1_b, w_router, w_gate, w_up, w_down, ln2_g, ln2_b):
    bsz, s, d = x.shape
    depth = w_ada.shape[0]
    ctx_row = bsz
    n_rows = -(-(bsz + 1) // SUBLANES) * SUBLANES
    cc = jnp.concatenate([c, c_ctx[None], jnp.zeros((n_rows - bsz - 1, d), F32)], axis=0)
    mod_all = _ada_all_layers(cc, w_ada, b_ada).reshape(depth, n_rows, 1, 6 * d)
    rope = _rope_tables(s)
    wl = _prep_params(w_in, g_q, w_uq, g_kv, w_ukv, rpb, w_dw, b_dw, g_cn, b_cn, w_pw2, w_oa, w_ob, w_out,
                      ln1_g, ln1_b, w_router, ln2_g, ln2_b)
    for l in range(depth):
        last = l == depth - 1
        qa, ka, va, qb, kb, vb, z, gl = _in_proj(x, mod_all, None, wl, l, rope, True)
        if last:
            ka_c, va_c, kb_c, vb_c = _in_proj(ctx, mod_all, ctx_row, wl, l, None, False)
        else:
            qa_c, ka_c, va_c, qb_c, kb_c, vb_c, z_c, gl_c = _in_proj(ctx, mod_all, ctx_row, wl, l, None, True)
        ya = _na_attention(qa, ka, va, ka_c, va_c, wl["na_tab"], l)
        yb = _mla_attention(qb, [(kb_c, vb_c), (kb, vb)])
        x1, h2, aff_t = _merge(x, ya, yb, z, gl, mod_all, None, wl, l)
        x = _moe(x1, h2, aff_t, mod_all, None, wl, w_gate, w_up, w_down, l, 1)
        if not last:
            ya_c = _mla_attention(qa_c, [(ka_c, va_c)], paired=True)
            yb_c = _mla_attention(qb_c, [(kb_c, vb_c)])
            c1, h2_c, aff_c = _merge(ctx, ya_c, yb_c, z_c, gl_c, mod_all, ctx_row, wl, l)
            ctx = _moe(c1, h2_c, aff_c, mod_all, ctx_row, wl, w_gate, w_up, w_down, l, bsz)
    return x
```

```python
import functools

import jax
import jax.numpy as jnp
from jax import lax
from jax.experimental import pallas as pl
from jax.experimental.pallas import tpu as pltpu

F32 = jnp.float32
BF16 = jnp.bfloat16

D_MODEL = 1024
DEPTH = 4
GRID_W = 64
NA_HEADS = 8
NA_HEAD_DIM = 64
NA_WIN_R = 8
NA_WIN_C = 16
NA_WIDTH = NA_HEADS * NA_HEAD_DIM
MLA_HEADS = 8
MLA_Q_RANK = 256
MLA_KV_RANK = 128
MLA_NOPE_DIM = 64
MLA_ROPE_DIM = 32
MLA_V_DIM = 64
ROPE_BASE = 10000.0
CONV_DIM = 512
CONV_WIDTH = 31
N_EXPERTS = 16
EXPERT_DIM = 1024
EC_CAPACITY_FACTOR = 2
LN_EPS = 1e-5
RMS_EPS = 1e-6
DEEPNORM_ALPHA = (2 * DEPTH) ** 0.25

LANES = 128
SUBLANES = 8
VMEM_LIMIT_BYTES = 56 * 1024 * 1024

ROW_TILE = 512
NA_Q_ROWS = 4
NA_BAND_ROWS = NA_Q_ROWS + NA_WIN_R
ROUTE_ROWS = 1024
EXPERT_SAMPLES = 2
CONV_CHUNK = 64
CONV_HALO = 16
HEAD_LANES = 128
MLA_KEY_CHUNK = 4096
MLA_Q_TILE = 1024
NEG_BIG = -1e30
LOG2_E = 1.4426950408889634

C_QA, C_KA, C_VA = 0, 512, 1024
C_CQ, C_CKV, C_KR = 1536, 1792, 1920
C_CONV_A, C_CONV_G, C_GATES = 2048, 2560, 3072
IN_COLS_PAD = 6144
NA_TAB_LO = NA_BAND_ROWS - NA_WIN_R
NA_TAB_HI = NA_BAND_ROWS - NA_WIN_R
NA_TAB = 2 * NA_WIN_R - 1 + NA_TAB_LO + NA_TAB_HI


def _params(*sem):
    return pltpu.CompilerParams(dimension_semantics=sem, vmem_limit_bytes=VMEM_LIMIT_BYTES)


def _layer_spec(shape, layer, **kw):
    zeros = (0,) * len(shape)
    return pl.BlockSpec((None,) + tuple(shape), lambda *g: (layer,) + zeros, **kw)


def _dot(a, b):
    return jnp.dot(a, b, preferred_element_type=F32)


def _dot_nt(a, b):
    return lax.dot_general(a, b, (((1,), (1,)), ((), ())), preferred_element_type=F32)


def _layer_norm(x, g, b):
    mu = jnp.mean(x, axis=-1, keepdims=True)
    var = jnp.mean(jnp.square(x - mu), axis=-1, keepdims=True)
    return (x - mu) * lax.rsqrt(var + LN_EPS) * g + b


def _rms_norm(x, g):
    return x * lax.rsqrt(jnp.mean(jnp.square(x), axis=-1, keepdims=True) + RMS_EPS) * g


def _silu(x):
    return x * jax.nn.sigmoid(x)


def _ada_kernel(c_ref, w_ref, b_ref, o_ref):
    s = _silu(c_ref[...]).astype(BF16)
    o_ref[0] = _dot(s, w_ref[0].astype(BF16)) + b_ref[0]


def _ada_all_layers(cc, w_ada, b_ada):
    n_layers, d, cols = w_ada.shape
    rows = cc.shape[0]
    tn = 1536
    return pl.pallas_call(
        _ada_kernel,
        grid=(n_layers, cols // tn),
        in_specs=[pl.BlockSpec((rows, d), lambda l, j: (0, 0)),
                  pl.BlockSpec((1, d, tn), lambda l, j: (l, 0, j)),
                  pl.BlockSpec((1, 1, tn), lambda l, j: (l, 0, j))],
        out_specs=pl.BlockSpec((1, rows, tn), lambda l, j: (l, 0, j)),
        out_shape=jax.ShapeDtypeStruct((n_layers, rows, cols), F32),
        compiler_params=_params("arbitrary", "arbitrary"),
        name="ada_ln",
    )(cc, w_ada, b_ada.reshape(n_layers, 1, cols))


def _rope(x, c, sa, sb):
    return x * c + pltpu.roll(x, HEAD_LANES - 8, axis=1) * sa + pltpu.roll(x, 8, axis=1) * sb


def _conv_taps(u_ext, w_ref):
    half = CONV_WIDTH // 2
    rows = CONV_CHUNK + 2 * CONV_HALO
    out = []
    for t0 in range(0, u_ext.shape[0] - 2 * CONV_HALO, CONV_CHUNK):
        accs = []
        for cb in range(CONV_DIM // LANES):
            ls = slice(cb * LANES, (cb + 1) * LANES)
            win = u_ext[t0:t0 + rows, ls]
            a = jnp.zeros((CONV_CHUNK, LANES), F32)
            for r in range(SUBLANES):
                ph = win if r == 0 else pltpu.roll(win, rows - r, axis=0)
                for lo in range(r, CONV_HALO - half + CONV_WIDTH, SUBLANES):
                    k = lo - (CONV_HALO - half)
                    if k >= 0:
                        a = a + ph[lo - r:lo - r + CONV_CHUNK, :] * w_ref[k:k + 1, ls]
            accs.append(a)
        out.append(jnp.concatenate(accs, axis=1))
    return jnp.concatenate(out, axis=0)


def _store_values_t(vt_ref, v):
    v_t = v.T
    ones = jnp.ones((NA_HEAD_DIM, v_t.shape[1]), BF16)
    for hd in range(v_t.shape[0] // NA_HEAD_DIM):
        vt_ref[0, hd * HEAD_LANES:hd * HEAD_LANES + NA_HEAD_DIM, :] = v_t[hd * NA_HEAD_DIM:(hd + 1) * NA_HEAD_DIM, :].astype(BF16)
        vt_ref[0, hd * HEAD_LANES + NA_HEAD_DIM:(hd + 1) * HEAD_LANES, :] = ones


def _in_proj_kernel(*refs, use_rope, full):
    n_in = 8 + (3 if use_rope else 0) + (6 if full else 0)
    ins, outs = refs[:n_in], refs[n_in:]
    x_ref, mod_ref, w_ref, gq_ref, gkv_ref, wuq_ref, wuk_ref, wuv_ref = ins[:8]
    pos = 8
    if use_rope:
        rc_ref, rsa_ref, rsb_ref = ins[pos:pos + 3]
        pos += 3
    if full:
        xp_ref, xn_ref, wdw_ref, bdw_ref, gcn_ref, bcn_ref = ins[pos:pos + 6]
        qa_ref, ka_ref, va_ref, qb_ref, kb_ref, vb_ref, z_ref, gl_ref = outs
    else:
        ka_ref, va_ref, kb_ref, vb_ref = outs
    d = D_MODEL
    mod = mod_ref[0]
    modulate = lambda xv: (xv * (1.0 + mod[:, d:2 * d]) + mod[:, 0:d]).astype(BF16)
    h = modulate(x_ref[0])

    def proj(c0, width):
        return _dot(h, w_ref[:, c0:c0 + width])

    if full:
        i = pl.program_id(1)
        h_ext = jnp.concatenate([modulate(xp_ref[0]), h, modulate(xn_ref[0])], axis=0)
        u_ext = (_dot(h_ext, w_ref[:, C_CONV_A:C_CONV_A + CONV_DIM])
                 * jax.nn.sigmoid(_dot(h_ext, w_ref[:, C_CONV_G:C_CONV_G + CONV_DIM])))
        row = lax.broadcasted_iota(jnp.int32, u_ext.shape, 0)
        inside = ((row >= CONV_HALO) | (i > 0)) & ((row < u_ext.shape[0] - CONV_HALO) | (i < pl.num_programs(1) - 1))
        u_ext = jnp.where(inside, u_ext, 0.0)
        for t0 in range(0, h.shape[0], CONV_CHUNK):
            conv = _conv_taps(u_ext[t0:t0 + CONV_CHUNK + 2 * CONV_HALO, :], wdw_ref)
            z_ref[0, t0:t0 + CONV_CHUNK, :] = _silu(_layer_norm(conv + bdw_ref[...], gcn_ref[...], bcn_ref[...])).astype(BF16)
    ka_ref[0] = proj(C_KA, NA_WIDTH).astype(BF16)
    _store_values_t(va_ref, proj(C_VA, NA_WIDTH))
    if use_rope:
        rc, rsa, rsb = rc_ref[...], rsa_ref[...], rsb_ref[...]
    n_kv = _rms_norm(proj(C_CKV, MLA_KV_RANK), gkv_ref[...]).astype(BF16)
    kr = proj(C_KR, HEAD_LANES)
    if use_rope:
        kr = _rope(kr, rc, rsa, rsb)
    k_nope = _dot(n_kv, wuk_ref[...])
    for hd in range(MLA_HEADS):
        sl = slice(hd * HEAD_LANES, (hd + 1) * HEAD_LANES)
        kb_ref[0, :, sl] = (k_nope[:, sl] + kr).astype(BF16)
    _store_values_t(vb_ref, _dot(n_kv, wuv_ref[...]))
    if not full:
        return
    qa_ref[0] = (proj(C_QA, NA_WIDTH) * (LOG2_E * NA_HEAD_DIM ** -0.5)).astype(BF16)
    n_q = _rms_norm(proj(C_CQ, MLA_Q_RANK), gq_ref[...]).astype(BF16)
    q = _dot(n_q, wuq_ref[...])
    mla_scale = LOG2_E * (MLA_NOPE_DIM + MLA_ROPE_DIM) ** -0.5
    for hd in range(MLA_HEADS):
        sl = slice(hd * HEAD_LANES, (hd + 1) * HEAD_LANES)
        qh = q[:, sl]
        if use_rope:
            qh = _rope(qh, rc, rsa, rsb)
        qb_ref[0, :, sl] = (qh * mla_scale).astype(BF16)
    gl_ref[0] = jax.nn.sigmoid(proj(C_GATES, 3 * d)).astype(BF16)


def _in_proj(x, mod_all, mod_row, wl, layer, rope, full):
    bsz, n, d = x.shape
    tm = min(n, ROW_TILE)
    use_rope = rope is not None
    if mod_row is None:
        mod_map = lambda b, i: (layer, b, 0, 0)
    else:
        mod_map = lambda b, i: (layer, mod_row, 0, 0)
    const2 = lambda b, i: (0, 0)
    lspec = lambda shape, **kw: _layer_spec(shape, layer, **kw)
    row_spec = lambda w: pl.BlockSpec((1, tm, w), lambda b, i: (b, i, 0))
    in_specs = [row_spec(d),
                pl.BlockSpec((None, 1, 1, 2 * d), mod_map),
                lspec((d, IN_COLS_PAD), pipeline_mode=pl.Buffered(1)),
                lspec((1, MLA_Q_RANK)),
                lspec((1, MLA_KV_RANK)),
                lspec((MLA_Q_RANK, MLA_HEADS * HEAD_LANES)),
                lspec((MLA_KV_RANK, MLA_HEADS * HEAD_LANES)),
                lspec((MLA_KV_RANK, MLA_HEADS * MLA_V_DIM))]
    args = [x, mod_all, wl["w_in"], wl["g_q"], wl["g_kv"], wl["w_uq"], wl["w_uk"], wl["w_uv"]]
    if use_rope:
        in_specs += [pl.BlockSpec((tm, HEAD_LANES), lambda b, i: (i, 0))] * 3
        args += list(rope)
    if full:
        per_tile = tm // CONV_HALO
        last_halo = n // CONV_HALO - 1
        vec = lspec((1, CONV_DIM))
        in_specs += [pl.BlockSpec((1, CONV_HALO, d), lambda b, i: (b, jnp.maximum(i * per_tile - 1, 0), 0)),
                     pl.BlockSpec((1, CONV_HALO, d), lambda b, i: (b, jnp.minimum((i + 1) * per_tile, last_halo), 0)),
                     lspec((CONV_WIDTH, CONV_DIM)), vec, vec, vec]
        args += [x, x, wl["w_dw"], wl["b_dw"], wl["g_cn"], wl["b_cn"]]
    sd = lambda w, dt: jax.ShapeDtypeStruct((bsz, n, w), dt)
    vt_shape = jax.ShapeDtypeStruct((bsz, MLA_HEADS * HEAD_LANES, n), BF16)
    vt_spec = pl.BlockSpec((1, MLA_HEADS * HEAD_LANES, tm), lambda b, i: (b, 0, i))
    kv_shapes = [sd(NA_WIDTH, BF16), vt_shape, sd(MLA_HEADS * HEAD_LANES, BF16), vt_shape]
    kv_specs = [row_spec(NA_WIDTH), vt_spec, row_spec(MLA_HEADS * HEAD_LANES), vt_spec]
    if full:
        out_shape = [sd(NA_WIDTH, BF16), kv_shapes[0], kv_shapes[1], sd(MLA_HEADS * HEAD_LANES, BF16), kv_shapes[2],
                     kv_shapes[3], sd(CONV_DIM, BF16), sd(3 * d, BF16)]
        out_specs = [row_spec(NA_WIDTH), kv_specs[0], kv_specs[1], row_spec(MLA_HEADS * HEAD_LANES), kv_specs[2],
                     kv_specs[3], row_spec(CONV_DIM), row_spec(3 * d)]
    else:
        out_shape, out_specs = kv_shapes, kv_specs
    return pl.pallas_call(
        functools.partial(_in_proj_kernel, use_rope=use_rope, full=full),
        grid=(bsz, n // tm),
        in_specs=in_specs,
        out_specs=out_specs,
        out_shape=out_shape,
        compiler_params=_params("parallel", "arbitrary"),
        name="in_proj",
    )(*args)


def _pair_lane_mask(rows):
    return lax.broadcasted_iota(jnp.int32, (rows, LANES), 1) < NA_HEAD_DIM


def _na_kernel(q_ref, k_ref, vt0_ref, vt1_ref, vt2_ref, kc_ref, vct_ref, tab_ref, o_ref, *, grid_rows):
    i = pl.program_id(1)
    tq = NA_Q_ROWS * GRID_W
    nband = NA_BAND_ROWS * GRID_W
    r0 = jnp.clip(NA_Q_ROWS * i - NA_WIN_R // 2, 0, grid_rows - NA_BAND_ROWS)
    kstart = pl.multiple_of(r0 * GRID_W, GRID_W)
    krow = r0 + lax.broadcasted_iota(jnp.int32, (nband, tq), 0) // GRID_W
    qrow = NA_Q_ROWS * i + lax.broadcasted_iota(jnp.int32, (nband, tq), 1) // GRID_W
    rstart = jnp.clip(qrow - NA_WIN_R // 2, 0, grid_rows - NA_WIN_R)
    row_mask = jnp.where((krow >= rstart) & (krow < rstart + NA_WIN_R), 0.0, NEG_BIG)
    tab_off = r0 - NA_Q_ROWS * i + (NA_WIN_R - 1) + NA_TAB_LO
    first = _pair_lane_mask(tq)
    vt_refs = (vt0_ref, vt1_ref, vt2_ref)
    vt_cols = vt0_ref.shape[2]

    def scores_of(hd):
        p, hh = divmod(hd, 2)
        ls = slice(p * LANES, (p + 1) * LANES)
        qp = q_ref[0, :, ls]
        qm = jnp.where(first if hh == 0 else jnp.logical_not(first), qp, jnp.zeros_like(qp))
        return _dot_nt(k_ref[0, pl.ds(kstart, nband), ls], qm), _dot_nt(kc_ref[0, :, ls], qm)

    def finish(hd, sw, sc):
        hs = slice(hd * HEAD_LANES, (hd + 1) * HEAD_LANES)
        bias = jnp.concatenate(
            [jnp.concatenate([tab_ref[hd * NA_TAB + tab_off + j - 2 * a2] for a2 in range(NA_Q_ROWS // 2)], axis=1)
             for j in range(NA_BAND_ROWS)], axis=0)
        sw = sw + (bias + row_mask)
        mx = jnp.maximum(jnp.max(sw, axis=0, keepdims=True), jnp.max(sc, axis=0, keepdims=True))
        pw = jnp.exp2(sw - mx).astype(BF16)
        o_t = _dot(vct_ref[0, hs, :], jnp.exp2(sc - mx).astype(BF16))
        for m, vt_ref in enumerate(vt_refs):
            o_t = o_t + _dot(vt_ref[0, hs, :], pw[m * vt_cols:(m + 1) * vt_cols, :])
        return o_t[0:NA_HEAD_DIM, :] / o_t[NA_HEAD_DIM:HEAD_LANES, :]

    outs = []
    nxt = scores_of(0)
    for hd in range(NA_HEADS):
        cur = nxt
        if hd + 1 < NA_HEADS:
            nxt = scores_of(hd + 1)
        outs.append(finish(hd, *cur))
    for p in range(NA_HEADS // 2):
        o_ref[0, :, p * LANES:(p + 1) * LANES] = jnp.concatenate(outs[2 * p:2 * p + 2], axis=0).T.astype(BF16)


def _na_attention(qa, ka, va_t, ka_c, va_t_c, tab, layer):
    bsz, s, w = qa.shape
    n_ctx = ka_c.shape[1]
    grid_rows = s // GRID_W
    assert grid_rows >= NA_BAND_ROWS and grid_rows % NA_Q_ROWS == 0 and NA_BAND_ROWS % NA_Q_ROWS == 0
    tq = NA_Q_ROWS * GRID_W
    n_vt = NA_BAND_ROWS // NA_Q_ROWS
    last_blk = grid_rows // NA_Q_ROWS - n_vt

    def vt_spec(m):
        return pl.BlockSpec((1, va_t.shape[1], tq), lambda b, i: (b, 0, jnp.clip(i - 1, 0, last_blk) + m))

    assert NA_WIN_R // 2 == NA_Q_ROWS and n_vt == 3
    return pl.pallas_call(
        functools.partial(_na_kernel, grid_rows=grid_rows),
        grid=(bsz, s // tq),
        in_specs=[pl.BlockSpec((1, tq, w), lambda b, i: (b, i, 0)),
                  pl.BlockSpec((1, s, w), lambda b, i: (b, 0, 0)),
                  vt_spec(0), vt_spec(1), vt_spec(2),
                  pl.BlockSpec((1, n_ctx, w), lambda b, i: (b, 0, 0)),
                  pl.BlockSpec((1, va_t_c.shape[1], n_ctx), lambda b, i: (b, 0, 0)),
                  _layer_spec(tab.shape[1:], layer, pipeline_mode=pl.Buffered(1))],
        out_specs=pl.BlockSpec((1, tq, w), lambda b, i: (b, i, 0)),
        out_shape=jax.ShapeDtypeStruct((bsz, s, w), BF16),
        compiler_params=_params("parallel", "arbitrary"),
        name="na_attn",
    )(qa, ka, va_t, va_t, va_t, ka_c, va_t_c, tab)


def _mla_attn_kernel(*refs, n_seg, paired):
    q_ref, kv_refs, o_ref = refs[0], refs[1:1 + 2 * n_seg], refs[1 + 2 * n_seg]
    first = _pair_lane_mask(q_ref.shape[1])
    chunks = []
    for j in range(n_seg):
        nk = kv_refs[2 * j].shape[1]
        step = nk if nk <= MLA_KEY_CHUNK else MLA_KEY_CHUNK
        chunks += [(j, c0, step) for c0 in range(0, nk, step)]

    def scores_of(hd):
        hs = slice(hd * HEAD_LANES, (hd + 1) * HEAD_LANES)
        if paired:
            hs = slice((hd // 2) * LANES, (hd // 2 + 1) * LANES)
            qp = q_ref[0, :, hs]
            qh = jnp.where(first if hd % 2 == 0 else jnp.logical_not(first), qp, jnp.zeros_like(qp))
        else:
            qh = q_ref[0, :, hs]
        return [_dot_nt(kv_refs[2 * j][0, c0:c0 + sz, hs], qh) for j, c0, sz in chunks]

    def finish(hd, scores):
        hs = slice(hd * HEAD_LANES, (hd + 1) * HEAD_LANES)
        mx = functools.reduce(jnp.maximum, [jnp.max(sj, axis=0, keepdims=True) for sj in scores])
        o_t = functools.reduce(lambda a, b: a + b,
                               [_dot(kv_refs[2 * j + 1][0, hs, c0:c0 + sz], jnp.exp2(sj - mx).astype(BF16))
                                for (j, c0, sz), sj in zip(chunks, scores)])
        return o_t[0:MLA_V_DIM, :] / o_t[MLA_V_DIM:HEAD_LANES, :]

    outs = []
    nxt = scores_of(0)
    for hd in range(MLA_HEADS):
        cur = nxt
        if hd + 1 < MLA_HEADS:
            nxt = scores_of(hd + 1)
        outs.append(finish(hd, cur))
    for p in range(MLA_HEADS // 2):
        o_ref[0, :, p * LANES:(p + 1) * LANES] = jnp.concatenate(outs[2 * p:2 * p + 2], axis=0).T.astype(BF16)


def _mla_attention(q, kvs, paired=False):
    bsz, nq, wq = q.shape
    tq = min(nq, MLA_Q_TILE)
    in_specs = [pl.BlockSpec((1, tq, wq), lambda b, i: (b, i, 0))]
    args = [q]
    for k, v_t in kvs:
        in_specs += [pl.BlockSpec((1, k.shape[1], k.shape[2]), lambda b, i: (b, 0, 0)),
                     pl.BlockSpec((1, v_t.shape[1], v_t.shape[2]), lambda b, i: (b, 0, 0))]
        args += [k, v_t]
    return pl.pallas_call(
        functools.partial(_mla_attn_kernel, n_seg=len(kvs), paired=paired),
        grid=(bsz, nq // tq),
        in_specs=in_specs,
        out_specs=pl.BlockSpec((1, tq, MLA_HEADS * MLA_V_DIM), lambda b, i: (b, i, 0)),
        out_shape=jax.ShapeDtypeStruct((bsz, nq, MLA_HEADS * MLA_V_DIM), BF16),
        compiler_params=_params("parallel", "arbitrary"),
        name="mla_attn",
    )(*args)


def _merge_kernel(x_ref, ya_ref, yb_ref, z_ref, gl_ref, g1_ref, sh2_ref, sc2_ref, woa_ref, wob_ref, wpw_ref, wout_ref,
                  g_ref, b_ref, wrt_ref, x1_ref, h2_ref, aff_ref, y_s):
    d = D_MODEL
    s = pl.program_id(0)

    @pl.when(s == 0)
    def _():
        y_s[...] = jnp.zeros(y_s.shape, F32)

    prev = y_s[...]
    gate = lambda br: gl_ref[0, :, br * d:(br + 1) * d].astype(F32)
    m = (gate(0) * _dot(ya_ref[0], woa_ref[...]) + gate(1) * _dot(yb_ref[0], wob_ref[...])
         + gate(2) * _dot(z_ref[0], wpw_ref[...]))
    y_s[...] = _dot(m.astype(BF16), wout_ref[...])
    x1 = _layer_norm(DEEPNORM_ALPHA * x_ref[0] + g1_ref[0] * prev, g_ref[...], b_ref[...])
    x1_ref[0] = x1
    h2 = (x1 * (1.0 + sc2_ref[0]) + sh2_ref[0]).astype(BF16)
    h2_ref[0] = h2
    logits = _dot_nt(wrt_ref[...], h2)
    e = jnp.exp(logits - jnp.max(logits, axis=0, keepdims=True))
    aff_ref[0] = e / jnp.sum(e, axis=0, keepdims=True)


def _merge(x, ya, yb, z, gl, mod_all, mod_row, wl, layer):
    bsz, n, d = x.shape
    tm = min(n, ROW_TILE)
    tiles = n // tm
    n_tiles = bsz * tiles
    mm = lambda s: jnp.minimum(s, n_tiles - 1)
    fin = lambda s: jnp.maximum(s - 1, 0)
    row_of = (lambda s: fin(s) // tiles) if mod_row is None else (lambda s: mod_row)
    mod_col = lambda j: pl.BlockSpec((None, 1, 1, d), lambda s: (layer, row_of(s), 0, j))
    cur = lambda w: pl.BlockSpec((1, tm, w), lambda s: (mm(s) // tiles, mm(s) % tiles, 0))
    done = lambda w: pl.BlockSpec((1, tm, w), lambda s: (fin(s) // tiles, fin(s) % tiles, 0))
    lspec = lambda shape: _layer_spec(shape, layer)
    return pl.pallas_call(
        _merge_kernel,
        grid=(n_tiles + 1,),
        in_specs=[done(d), cur(NA_WIDTH), cur(NA_WIDTH), cur(CONV_DIM), cur(3 * d),
                  mod_col(2), mod_col(3), mod_col(4),
                  lspec((NA_WIDTH, d)), lspec((NA_WIDTH, d)), lspec((CONV_DIM, d)), lspec((d, d)),
                  lspec((1, d)), lspec((1, d)), lspec((N_EXPERTS, d))],
        out_specs=[done(d), done(d),
                   pl.BlockSpec((1, N_EXPERTS, tm), lambda s: (fin(s) // tiles, 0, fin(s) % tiles))],
        out_shape=[jax.ShapeDtypeStruct((bsz, n, d), F32), jax.ShapeDtypeStruct((bsz, n, d), BF16),
                   jax.ShapeDtypeStruct((bsz, N_EXPERTS, n), F32)],
        scratch_shapes=[pltpu.VMEM((tm, d), F32)],
        compiler_params=_params("arbitrary"),
        name="merge_ln1_router",
    )(x, ya, yb, z, gl, mod_all, mod_all, mod_all, wl["w_oa"], wl["w_ob"], wl["w_pw2"], wl["w_out"],
      wl["ln1_g"], wl["ln1_b"], wl["w_rt"])


def _lane_cumsum(v):
    n = v.shape[1]
    lane = lax.broadcasted_iota(jnp.int32, v.shape, 1)
    s = 1
    while s < n:
        v = v + jnp.where(lane >= s, pltpu.roll(v, s, axis=1), 0)
        s *= 2
    return v


def _route_kernel(aff_ref, slot_ref, slot_t_ref, *, cap):
    nb, e, n = aff_ref.shape
    bits = pltpu.bitcast(aff_ref[...].reshape(nb * e, n), jnp.int32)

    def search(it, thr):
        cand = thr | lax.shift_left(jnp.int32(1), 30 - it)
        cnt = jnp.sum(jnp.where(bits >= cand, 1.0, 0.0), axis=1, keepdims=True)
        return jnp.where(cnt >= cap, cand, thr)

    thr = lax.fori_loop(0, 31, search, jnp.zeros((nb * e, 1), jnp.int32))
    gt = jnp.where(bits > thr, 1, 0)
    eq = jnp.where(bits == thr, 1, 0)
    inc = _lane_cumsum(gt + eq * 65536)
    gt_before = (inc & 65535) - gt
    eq_before = lax.shift_right_logical(inc, 16) - eq
    need = cap - jnp.sum(gt, axis=1, keepdims=True)
    sel = (gt == 1) | ((eq == 1) & (eq_before < need))
    slot = jnp.where(sel, gt_before + jnp.minimum(eq_before, need), -1)
    slot_ref[...] = slot.reshape(nb, e, n)
    filler = jnp.full((LANES - e, n), -1.0, F32)
    for j in range(nb):
        padded = jnp.concatenate([slot[j * e:(j + 1) * e, :].astype(F32), filler], axis=0)
        slot_t_ref[j] = padded.T.astype(jnp.int32)


def _route(aff_t, cap):
    bsz, e, n = aff_t.shape
    nb = min(bsz, max(1, ROUTE_ROWS // (e * n // LANES)))
    return pl.pallas_call(
        functools.partial(_route_kernel, cap=cap),
        grid=(bsz // nb,),
        in_specs=[pl.BlockSpec((nb, e, n), lambda b: (b, 0, 0))],
        out_specs=[pl.BlockSpec((nb, e, n), lambda b: (b, 0, 0)), pl.BlockSpec((nb, n, LANES), lambda b: (b, 0, 0))],
        out_shape=[jax.ShapeDtypeStruct((bsz, e, n), jnp.int32), jax.ShapeDtypeStruct((bsz, n, LANES), jnp.int32)],
        compiler_params=_params("parallel"),
        name="route",
    )(aff_t)


def _expert_kernel(slot_ref, aff_ref, h_ref, wg_ref, wu_ref, wd_ref, y_ref, wg_s, wu_s, wd_s, xg_s, ac_s, *, cap):
    @pl.when(pl.program_id(1) == 0)
    def _():
        wg_s[...] = wg_ref[0, 0].astype(BF16)
        wu_s[...] = wu_ref[0, 0].astype(BF16)
        wd_s[...] = wd_ref[0, 0].astype(BF16)

    nb, n = h_ref.shape[0], h_ref.shape[1]
    c_iota = lax.broadcasted_iota(jnp.int32, (cap, n), 0)
    for j in range(nb):
        hit = slot_ref[j, 0] == c_iota
        onehot = jnp.where(hit, 1.0, 0.0).astype(BF16)
        xg_s[j * cap:(j + 1) * cap, :] = _dot(onehot, h_ref[j]).astype(BF16)
        ac_s[j * cap:(j + 1) * cap, :] = jnp.sum(jnp.where(hit, aff_ref[j, 0], 0.0), axis=1, keepdims=True)
    xg = xg_s[...]
    hid = _silu(_dot(xg, wg_s[...])) * _dot(xg, wu_s[...])
    y = _dot(hid.astype(BF16), wd_s[...]) * ac_s[...]
    for j in range(nb):
        y_ref[j, 0] = y[j * cap:(j + 1) * cap, :].astype(BF16)


def _experts(slot, aff_t, h2, w_gate, w_up, w_down, layer, cap, nb):
    bsz, e, n = slot.shape
    d, f = w_gate.shape[2], w_gate.shape[3]
    slot4 = slot.reshape(bsz, e, 1, n)
    aff4 = aff_t.reshape(bsz, e, 1, n)
    return pl.pallas_call(
        functools.partial(_expert_kernel, cap=cap),
        grid=(e, bsz // nb),
        in_specs=[pl.BlockSpec((nb, 1, 1, n), lambda ei, bi: (bi, ei, 0, 0)),
                  pl.BlockSpec((nb, 1, 1, n), lambda ei, bi: (bi, ei, 0, 0)),
                  pl.BlockSpec((nb, n, d), lambda ei, bi: (bi, 0, 0)),
                  pl.BlockSpec((1, 1, d, f), lambda ei, bi: (layer, ei, 0, 0)),
                  pl.BlockSpec((1, 1, d, f), lambda ei, bi: (layer, ei, 0, 0)),
                  pl.BlockSpec((1, 1, f, d), lambda ei, bi: (layer, ei, 0, 0))],
        out_specs=pl.BlockSpec((nb, 1, cap, d), lambda ei, bi: (bi, ei, 0, 0)),
        out_shape=jax.ShapeDtypeStruct((bsz, e, cap, d), BF16),
        scratch_shapes=[pltpu.VMEM((d, f), BF16), pltpu.VMEM((d, f), BF16), pltpu.VMEM((f, d), BF16),
                        pltpu.VMEM((nb * cap, d), BF16), pltpu.VMEM((nb * cap, 1), F32)],
        compiler_params=_params("arbitrary", "arbitrary"),
        name="experts",
    )(slot4, aff4, h2, w_gate, w_up, w_down)


def _combine_kernel(st_ref, y_ref, x_ref, mod_ref, g_ref, b_ref, o_ref, moe_s, *, cap):
    s = pl.program_id(0)

    @pl.when(s == 0)
    def _():
        moe_s[...] = jnp.zeros(moe_s.shape, F32)

    prev = moe_s[...]
    st = st_ref[0]
    rows = st.shape[0]
    if cap % LANES == 0:
        c_iota = lax.broadcasted_iota(jnp.int32, (rows, cap), 1)
        scat = jnp.concatenate(
            [jnp.where(st[:, e:e + 1] == c_iota, 1.0, 0.0).astype(BF16) for e in range(N_EXPERTS)], axis=1)
    else:
        j_iota = lax.broadcasted_iota(jnp.int32, (rows, N_EXPERTS * cap), 1)
        hit = None
        for e in range(N_EXPERTS):
            col = st[:, e:e + 1]
            he = jnp.where(col >= 0, col + e * cap, -1) == j_iota
            hit = he if hit is None else (hit | he)
        scat = jnp.where(hit, 1.0, 0.0).astype(BF16)
    moe_s[...] = _dot(scat, y_ref[0])
    o_ref[0] = _layer_norm(DEEPNORM_ALPHA * x_ref[0] + mod_ref[0] * prev, g_ref[...], b_ref[...])


def _combine(slot_t, y, x1, mod_all, mod_row, wl, layer, cap):
    bsz, n, d = x1.shape
    tm = min(n, ROW_TILE)
    tiles = n // tm
    n_tiles = bsz * tiles
    mm = lambda s: jnp.minimum(s, n_tiles - 1)
    fin = lambda s: jnp.maximum(s - 1, 0)
    row_of = (lambda s: fin(s) // tiles) if mod_row is None else (lambda s: mod_row)
    ec = N_EXPERTS * cap
    return pl.pallas_call(
        functools.partial(_combine_kernel, cap=cap),
        grid=(n_tiles + 1,),
        in_specs=[pl.BlockSpec((1, tm, LANES), lambda s: (mm(s) // tiles, mm(s) % tiles, 0)),
                  pl.BlockSpec((1, ec, d), lambda s: (mm(s) // tiles, 0, 0)),
                  pl.BlockSpec((1, tm, d), lambda s: (fin(s) // tiles, fin(s) % tiles, 0)),
                  pl.BlockSpec((None, 1, 1, d), lambda s: (layer, row_of(s), 0, 5)),
                  _layer_spec((1, d), layer), _layer_spec((1, d), layer)],
        out_specs=pl.BlockSpec((1, tm, d), lambda s: (fin(s) // tiles, fin(s) % tiles, 0)),
        out_shape=jax.ShapeDtypeStruct((bsz, n, d), F32),
        scratch_shapes=[pltpu.VMEM((tm, d), F32)],
        compiler_params=_params("arbitrary"),
        name="combine_ln2",
    )(slot_t, y.reshape(bsz, ec, d), x1, mod_all, wl["ln2_g"], wl["ln2_b"])


def _rope_tables(n_tokens):
    t = jnp.arange(n_tokens)
    row = (t // GRID_W).astype(F32)
    col = (t % GRID_W).astype(F32)
    n_freq = MLA_ROPE_DIM // 4
    inv_freq = ROPE_BASE ** (-jnp.arange(n_freq, dtype=F32) / n_freq)
    ar, ac = row[:, None] * inv_freq, col[:, None] * inv_freq
    z8 = jnp.zeros((n_tokens, n_freq), F32)
    ones = lambda w: jnp.ones((n_tokens, w), F32)
    zeros = lambda w: jnp.zeros((n_tokens, w), F32)
    cos_r, sin_r, cos_c, sin_c = jnp.cos(ar), jnp.sin(ar), jnp.cos(ac), jnp.sin(ac)
    tail = HEAD_LANES - MLA_NOPE_DIM - MLA_ROPE_DIM
    rc = jnp.concatenate([ones(MLA_NOPE_DIM), cos_r, cos_r, cos_c, cos_c, ones(tail)], axis=1)
    rsa = jnp.concatenate([zeros(MLA_NOPE_DIM), -sin_r, z8, -sin_c, z8, zeros(tail)], axis=1)
    rsb = jnp.concatenate([zeros(MLA_NOPE_DIM), z8, sin_r, z8, sin_c, zeros(tail)], axis=1)
    return rc, rsa, rsb


def _na_bias_table(rpb):
    col = jnp.arange(GRID_W)
    c_start = jnp.clip(col - NA_WIN_C // 2, 0, GRID_W - NA_WIN_C)
    col_mask = (col[None, :] >= c_start[:, None]) & (col[None, :] < c_start[:, None] + NA_WIN_C)
    dc_idx = jnp.clip(col[None, :] - col[:, None], -(NA_WIN_C - 1), NA_WIN_C - 1) + NA_WIN_C - 1
    blocks = jnp.where(col_mask, rpb[..., dc_idx] * LOG2_E, NEG_BIG)
    blocks = jnp.swapaxes(blocks, -1, -2)
    blocks = jnp.pad(blocks, ((0, 0), (0, 0), (NA_TAB_LO + 1, NA_TAB_HI), (0, 0), (0, 0)), constant_values=NEG_BIG)
    pairs = jnp.concatenate([blocks[:, :, 1:], blocks[:, :, :-1]], axis=-1)
    return pairs.reshape(rpb.shape[0], NA_HEADS * NA_TAB, GRID_W, 2 * GRID_W).astype(F32)


def _prep_params(w_in, g_q, w_uq, g_kv, w_ukv, rpb, w_dw, b_dw, g_cn, b_cn, w_pw2, w_oa, w_ob, w_out,
                 ln1_g, ln1_b, w_router, ln2_g, ln2_b):
    n_layers, d = w_in.shape[0], D_MODEL
    off = [0, 512, 1024, 1536, 1792, 1920, 1952, 2976, 6048]
    zpad = lambda w: jnp.zeros((n_layers, d, w), F32)
    w_main = jnp.concatenate([
        w_in[:, :, off[0]:off[5]],
        zpad(MLA_NOPE_DIM), w_in[:, :, off[5]:off[6]], zpad(HEAD_LANES - MLA_NOPE_DIM - MLA_ROPE_DIM),
        w_in[:, :, off[6]:off[8]],
    ], axis=2).astype(BF16)
    hq = MLA_NOPE_DIM + MLA_ROPE_DIM
    wq = w_uq.reshape(n_layers, MLA_Q_RANK, MLA_HEADS, hq)
    wq = jnp.pad(wq, ((0, 0), (0, 0), (0, 0), (0, HEAD_LANES - hq))).reshape(n_layers, MLA_Q_RANK, MLA_HEADS * HEAD_LANES)
    wkv = w_ukv.reshape(n_layers, MLA_KV_RANK, MLA_HEADS, MLA_NOPE_DIM + MLA_V_DIM)
    wk = jnp.pad(wkv[..., :MLA_NOPE_DIM], ((0, 0), (0, 0), (0, 0), (0, HEAD_LANES - MLA_NOPE_DIM)))
    wv = wkv[..., MLA_NOPE_DIM:]
    vec = lambda v: v[:, None, :]
    return dict(
        w_in=w_main, g_q=vec(g_q), g_kv=vec(g_kv),
        w_uq=wq.astype(BF16), w_uk=wk.reshape(n_layers, MLA_KV_RANK, MLA_HEADS * HEAD_LANES).astype(BF16),
        w_uv=wv.reshape(n_layers, MLA_KV_RANK, MLA_HEADS * MLA_V_DIM).astype(BF16),
        na_tab=_na_bias_table(rpb),
        w_dw=w_dw, b_dw=vec(b_dw), g_cn=vec(g_cn), b_cn=vec(b_cn),
        w_pw2=w_pw2.astype(BF16), w_oa=w_oa.astype(BF16), w_ob=w_ob.astype(BF16), w_out=w_out.astype(BF16),
        ln1_g=vec(ln1_g), ln1_b=vec(ln1_b), w_rt=jnp.swapaxes(w_router, 1, 2).astype(BF16),
        ln2_g=vec(ln2_g), ln2_b=vec(ln2_b))


def _moe(x1, h2, aff_t, mod_all, mod_row, wl, w_gate, w_up, w_down, layer, nb):
    n = x1.shape[1]
    cap = EC_CAPACITY_FACTOR * n // N_EXPERTS
    slot, slot_t = _route(aff_t, cap)
    y = _experts(slot, aff_t, h2, w_gate, w_up, w_down, layer, cap, nb)
    return _combine(slot_t, y, x1, mod_all, mod_row, wl, layer, cap)


def kernel(x, c, ctx, c_ctx, w_ada, b_ada, w_in, g_q, w_uq, g_kv, w_ukv, rpb, w_dw, b_dw, g_cn, b_cn, w_pw2, w_oa,
           w_ob, w_out, ln1_g, ln1_b, w_router, w_gate, w_up, w_down, ln2_g, ln2_b):
    bsz, s, d = x.shape
    depth = w_ada.shape[0]
    ctx_row = bsz
    n_rows = -(-(bsz + 1) // SUBLANES) * SUBLANES
    cc = jnp.concatenate([c, c_ctx[None], jnp.zeros((n_rows - bsz - 1, d), F32)], axis=0)
    mod_all = _ada_all_layers(cc, w_ada, b_ada).reshape(depth, n_rows, 1, 6 * d)
    rope = _rope_tables(s)
    wl = _prep_params(w_in, g_q, w_uq, g_kv, w_ukv, rpb, w_dw, b_dw, g_cn, b_cn, w_pw2, w_oa, w_ob, w_out,
                      ln1_g, ln1_b, w_router, ln2_g, ln2_b)
    for l in range(depth):
        last = l == depth - 1
        qa, ka, va, qb, kb, vb, z, gl = _in_proj(x, mod_all, None, wl, l, rope, True)
        if last:
            ka_c, va_c, kb_c, vb_c = _in_proj(ctx, mod_all, ctx_row, wl, l, None, False)
        else:
            qa_c, ka_c, va_c, qb_c, kb_c, vb_c, z_c, gl_c = _in_proj(ctx, mod_all, ctx_row, wl, l, None, True)
        ya = _na_attention(qa, ka, va, ka_c, va_c, wl["na_tab"], l)
        yb = _mla_attention(qb, [(kb_c, vb_c), (kb, vb)])
        x1, h2, aff_t = _merge(x, ya, yb, z, gl, mod_all, None, wl, l)
        x = _moe(x1, h2, aff_t, mod_all, None, wl, w_gate, w_up, w_down, l, EXPERT_SAMPLES)
        if not last:
            ya_c = _mla_attention(qa_c, [(ka_c, va_c)], paired=True)
            yb_c = _mla_attention(qb_c, [(kb_c, vb_c)])
            c1, h2_c, aff_c = _merge(ctx, ya_c, yb_c, z_c, gl_c, mod_all, ctx_row, wl, l)
            ctx = _moe(c1, h2_c, aff_c, mod_all, ctx_row, wl, w_gate, w_up, w_down, l, bsz)
    return x
```
